```python
import math
import jax
import jax.numpy as jnp
from jax import lax
import numpy as np

D_MODEL = 1024
BATCH = 2
SEQ = 16384
DEPTH = 2

MEM_LEN = 256
EPS = 1e-6
MASK_VALUE = -1e30
TINY = 1e-30

A_HEADS = 8
A_NOPE = 64
A_ROPE = 32
A_V = 64
A_Q_RANK = 384
A_KV_RANK = 256
A_QBLOCK = 128
ROPE_THETA = 10000.0

B_HEADS = 8
B_DK = 128
B_DV = 64
B_CHUNK = 16

C_HEADS = 8
C_KV_HEADS = 2
C_DH = 64
C_WINDOW = 128
C_BLOCK = 128

REL_BUCKETS = 32
REL_MAX_DIST = 128

X_HEADS = 4
X_DH = 256

D_FF = -(-(8 * D_MODEL) // (3 * 256)) * 256

IN_SPLITS = (
    A_Q_RANK, A_KV_RANK, A_ROPE,
    B_HEADS * B_DK, B_HEADS * B_DK, B_HEADS * B_DK,
    B_HEADS * B_DV, B_HEADS * B_DV,
    C_HEADS * C_DH, C_KV_HEADS * C_DH, C_KV_HEADS * C_DH,
    D_MODEL, D_MODEL, D_MODEL,
)
IN_WIDTH = sum(IN_SPLITS)

kernel_name = 'hybrid_mla_hgrn2_swa_encoder'


def _rmsnorm(x, g):
    x32 = x.astype(jnp.float32)
    y = x32 * lax.rsqrt(jnp.mean(x32 * x32, axis=-1, keepdims=True) + EPS)
    return (y * g.astype(jnp.float32)).astype(x.dtype)


def _split_cols(z, sizes):
    out, start = [], 0
    for s in sizes:
        out.append(z[..., start:start + s])
        start += s
    return out


def _rope(x, pos):
    half = x.shape[-1] // 2
    inv = ROPE_THETA ** (-jnp.arange(half, dtype=jnp.float32) / half)
    ang = pos.astype(jnp.float32)[:, None] * inv[None, :]
    cos = jnp.cos(ang)[None, :, None, :]
    sin = jnp.sin(ang)[None, :, None, :]
    x32 = x.astype(jnp.float32)
    x1, x2 = x32[..., :half], x32[..., half:]
    return jnp.concatenate([x1 * cos - x2 * sin, x1 * sin + x2 * cos], axis=-1).astype(x.dtype)


def _t5_bucket(rel):
    nb = REL_BUCKETS // 2
    max_exact = nb // 2
    ret = (rel > 0).astype(jnp.int32) * nb
    n = jnp.abs(rel)
    large = max_exact + (jnp.log(jnp.maximum(n, 1).astype(jnp.float32) / max_exact)
                         / math.log(REL_MAX_DIST / max_exact) * (nb - max_exact)).astype(jnp.int32)
    large = jnp.minimum(large, nb - 1)
    return ret + jnp.where(n < max_exact, n, large)


def _mla(cq, ckv, kr, gq, gkv, wuq, wukv, pos):
    Bsz, S, _ = cq.shape
    q = (_rmsnorm(cq, gq) @ wuq).reshape(Bsz, S, A_HEADS, A_NOPE + A_ROPE)
    q = jnp.concatenate([q[..., :A_NOPE], _rope(q[..., A_NOPE:], pos)], axis=-1)
    kv = (_rmsnorm(ckv, gkv) @ wukv).reshape(Bsz, S, A_HEADS, A_NOPE + A_V)
    k_rope = jnp.broadcast_to(_rope(kr[:, :, None, :], pos), (Bsz, S, A_HEADS, A_ROPE))
    k = jnp.concatenate([kv[..., :A_NOPE], k_rope], axis=-1)
    v = kv[..., A_NOPE:]
    scale = (A_NOPE + A_ROPE) ** -0.5
    nb = S // A_QBLOCK
    qb = jnp.moveaxis(q.reshape(Bsz, nb, A_QBLOCK, A_HEADS, A_NOPE + A_ROPE), 1, 0)

    def attend(q_blk):
        s = jnp.einsum('bqhd,bkhd->bhqk', q_blk, k).astype(jnp.float32) * scale
        p = jax.nn.softmax(s, axis=-1).astype(v.dtype)
        return jnp.einsum('bhqk,bkhd->bqhd', p, v)

    o = lax.map(attend, qb)
    return jnp.moveaxis(o, 0, 1).reshape(Bsz, S, A_HEADS * A_V)


def _gated_scan(q, k, v, log_f):
    Bsz, S, H, DK = q.shape
    DV = v.shape[-1]
    nc = S // B_CHUNK
    q, k, log_f = [t.reshape(Bsz, nc, B_CHUNK, H, DK) for t in (q, k, log_f)]
    v = v.reshape(Bsz, nc, B_CHUNK, H, DV)
    b = jnp.cumsum(log_f, axis=2)
    b_last = b[:, :, -1:]
    q_dec = q * jnp.exp(b)
    k_inv = k * jnp.exp(-b)
    k_end = k * jnp.exp(b_last - b)
    scores = jnp.einsum('bnthk,bnshk->bnhts', q_dec, k_inv)
    tri = jnp.tril(jnp.ones((B_CHUNK, B_CHUNK), dtype=bool))
    scores = jnp.where(tri, scores, 0.0)
    o_intra = jnp.einsum('bnhts,bnshv->bnthv', scores, v)

    def step(state, inp):
        q_c, k_c, v_c, dec_c = inp
        o_c = jnp.einsum('bthk,bhkv->bthv', q_c, state)
        state = state * dec_c[:, 0, :, :, None] + jnp.einsum('bshk,bshv->bhkv', k_c, v_c)
        return state, o_c

    xs = tuple(jnp.moveaxis(t, 1, 0) for t in (q_dec, k_end, v, jnp.exp(b_last)))
    s0 = jnp.zeros((Bsz, H, DK, DV), jnp.float32)
    _, o_inter = lax.scan(step, s0, xs)
    o = o_intra + jnp.moveaxis(o_inter, 0, 1)
    return o.reshape(Bsz, S, H, DV)


def _hgrn2(q, f_fwd, f_bwd, i, g, lb_fwd, lb_bwd, g_out):
    Bsz, S, _ = q.shape
    dt = q.dtype

    def heads(t, d):
        return t.astype(jnp.float32).reshape(Bsz, S, B_HEADS, d)

    def gates(z, lb):
        lb = lb.astype(jnp.float32).reshape(B_HEADS, B_DK)
        zh = heads(z, B_DK)
        f = lb + (1.0 - lb) * jax.nn.sigmoid(zh)
        log_f = jnp.log(jnp.maximum(f, TINY))
        key = (1.0 - lb) * jax.nn.sigmoid(-zh)
        return log_f, key

    qh = heads(q, B_DK)
    vh = heads(i, B_DV)
    lf_f, k_f = gates(f_fwd, lb_fwd)
    lf_b, k_b = gates(f_bwd, lb_bwd)
    o_f = _gated_scan(qh, k_f, vh, lf_f)
    flip = lambda t: jnp.flip(t, axis=1)
    o_b = flip(_gated_scan(flip(qh), flip(k_b), flip(vh), flip(lf_b)))
    o = _rmsnorm(o_f + o_b, g_out) * jax.nn.silu(heads(g, B_DV))
    return o.reshape(Bsz, S, B_HEADS * B_DV).astype(dt)


def _window_gqa(q, k, v, rel_bias, sink):
    Bsz, S, _ = q.shape
    nb = S // C_BLOCK
    G = C_HEADS // C_KV_HEADS
    span = 3 * C_BLOCK
    q = q.reshape(Bsz, nb, C_BLOCK, C_KV_HEADS, G, C_DH)

    def band(t):
        t = t.reshape(Bsz, S, C_KV_HEADS, C_DH)
        t = jnp.pad(t, ((0, 0), (C_BLOCK, C_BLOCK), (0, 0), (0, 0)))
        t = t.reshape(Bsz, nb + 2, C_BLOCK, C_KV_HEADS, C_DH)
        return jnp.concatenate([t[:, :-2], t[:, 1:-1], t[:, 2:]], axis=2)

    kb, vb = band(k), band(v)
    rel = jnp.arange(span)[None, :] - C_BLOCK - jnp.arange(C_BLOCK)[:, None]
    bias = rel_bias.astype(jnp.float32)[_t5_bucket(rel)]
    bias = jnp.transpose(bias, (2, 0, 1)).reshape(C_KV_HEADS, G, C_BLOCK, span)
    key_pos = (jnp.arange(nb)[:, None] - 1) * C_BLOCK + jnp.arange(span)[None, :]
    valid = (jnp.abs(rel) <= C_WINDOW)[None] & ((key_pos >= 0) & (key_pos < S))[:, None, :]
    s = jnp.einsum('bnqkgd,bnskd->bnkgqs', q, kb).astype(jnp.float32) * (C_DH ** -0.5) + bias
    s = jnp.where(valid[None, :, None, None], s, MASK_VALUE)
    sink_l = sink.astype(jnp.float32).reshape(C_KV_HEADS, G)[:, :, None, None]
    m = jnp.maximum(jnp.max(s, axis=-1, keepdims=True), sink_l)
    p = jnp.exp(s - m)
    p = p / (jnp.sum(p, axis=-1, keepdims=True) + jnp.exp(sink_l - m))
    o = jnp.einsum('bnkgqs,bnskd->bnqkgd', p.astype(v.dtype), vb)
    return o.reshape(Bsz, S, C_HEADS * C_DH)


def _cross(h, mem_n, wq, wkv, wo):
    Bsz, S, _ = h.shape
    q = (h @ wq).reshape(Bsz, S, X_HEADS, X_DH)
    kv = (mem_n @ wkv).reshape(Bsz, mem_n.shape[1], 2, X_HEADS, X_DH)
    s = jnp.einsum('bqhd,bkhd->bhqk', q, kv[:, :, 0]).astype(jnp.float32) * (X_DH ** -0.5)
    p = jax.nn.softmax(s, axis=-1).astype(h.dtype)
    o = jnp.einsum('bhqk,bkhd->bqhd', p, kv[:, :, 1]).reshape(Bsz, S, X_HEADS * X_DH)
    return o @ wo


def _swiglu(h, w1, w3, w2):
    return (jax.nn.silu(h @ w1) * (h @ w3)) @ w2


def setup_inputs(seed: int = 0) -> dict:
    key = jax.random.key(seed)
    ks = iter(jax.random.split(key, 32))
    f32 = jnp.float32

    def nrm(shape, fan_in):
        return jax.random.normal(next(ks), shape, f32) * (fan_in ** -0.5)

    def gain(shape):
        return 1.0 + 0.02 * jax.random.normal(next(ks), shape, f32)

    L, D = DEPTH, D_MODEL
    return {
        'x': jax.random.normal(next(ks), (BATCH, SEQ, D), f32),
        'mem': jax.random.normal(next(ks), (BATCH, MEM_LEN, D), f32),
        'w_in': nrm((L, D, IN_WIDTH), D),
        'g_mix': gain((L, D)),
        'a_gq': gain((L, A_Q_RANK)),
        'a_gkv': gain((L, A_KV_RANK)),
        'a_wuq': nrm((L, A_Q_RANK, A_HEADS * (A_NOPE + A_ROPE)), A_Q_RANK),
        'a_wukv': nrm((L, A_KV_RANK, A_HEADS * (A_NOPE + A_V)), A_KV_RANK),
        'b_lb': jax.random.normal(next(ks), (2, L, B_HEADS * B_DK), f32),
        'b_gout': gain((L, B_DV)),
        'c_sink': 0.5 * jax.random.normal(next(ks), (L, C_HEADS), f32),
        'rel_bias': 0.5 * jax.random.normal(next(ks), (REL_BUCKETS, C_HEADS), f32),
        'w_br_a': nrm((L, A_HEADS * A_V, D), A_HEADS * A_V),
        'w_br_b': nrm((L, B_HEADS * B_DV, D), B_HEADS * B_DV),
        'w_br_c': nrm((L, C_HEADS * C_DH, D), C_HEADS * C_DH),
        'w_out': nrm((L, D, D), D),
        'g_x': gain((L, D)),
        'g_mem': gain((L, D)),
        'x_wq': nrm((L, D, X_HEADS * X_DH), D),
        'x_wkv': nrm((L, D, 2 * X_HEADS * X_DH), D),
        'x_wo': nrm((L, X_HEADS * X_DH, D), X_HEADS * X_DH),
        'g_ffn': gain((L, D)),
        'f_w1': nrm((L, D, D_FF), D),
        'f_w3': nrm((L, D, D_FF), D),
        'f_w2': nrm((L, D_FF, D), D_FF),
        'g_final': gain((D,)),
    }


def reference(x, mem, w_in, g_mix, a_gq, a_gkv, a_wuq, a_wukv, b_lb, b_gout, c_sink, rel_bias,
              w_br_a, w_br_b, w_br_c, w_out, g_x, g_mem, x_wq, x_wkv, x_wo, g_ffn,
              f_w1, f_w3, f_w2, g_final):
    S = x.shape[1]
    pos = jnp.arange(S, dtype=jnp.int32)
    sm = jax.nn.softmax(b_lb.astype(jnp.float32), axis=1)
    lower_bounds = jnp.cumsum(sm, axis=1) - sm[:, :1]
    for l in range(DEPTH):
        h = _rmsnorm(x, g_mix[l])
        (a_cq, a_ckv, a_kr, b_q, b_ff, b_fb, b_i, b_g,
         c_q, c_k, c_v, gate_a, gate_b, gate_c) = _split_cols(h @ w_in[l], IN_SPLITS)
        y_a = _mla(a_cq, a_ckv, a_kr, a_gq[l], a_gkv[l], a_wuq[l], a_wukv[l], pos)
        y_b = _hgrn2(b_q, b_ff, b_fb, b_i, b_g, lower_bounds[0, l], lower_bounds[1, l], b_gout[l])
        y_c = _window_gqa(c_q, c_k, c_v, rel_bias, c_sink[l])
        merged = (jax.nn.sigmoid(gate_a) * (y_a @ w_br_a[l])
                  + jax.nn.sigmoid(gate_b) * (y_b @ w_br_b[l])
                  + jax.nn.sigmoid(gate_c) * (y_c @ w_br_c[l]))
        x = x + merged @ w_out[l]
        h = _rmsnorm(x, g_x[l])
        x = x + _cross(h, _rmsnorm(mem, g_mem[l]), x_wq[l], x_wkv[l], x_wo[l])
        h = _rmsnorm(x, g_ffn[l])
        x = x + _swiglu(h, f_w1[l], f_w3[l], f_w2[l])
    return _rmsnorm(x, g_final)
```

```python
import functools
import math

import jax
import jax.numpy as jnp
from jax import lax
from jax.experimental import pallas as pl
from jax.experimental.pallas import tpu as pltpu

F32 = jnp.float32
BF16 = jnp.bfloat16

EPS = 1e-6
MASK_VALUE = -1e30
TINY = 1e-30
ROPE_THETA = 10000.0

A_HEADS, A_NOPE, A_ROPE, A_V = 8, 64, 32, 64
A_Q_RANK, A_KV_RANK = 384, 256
B_HEADS, B_DK, B_DV = 8, 128, 64
C_HEADS, C_KV_HEADS, C_DH, C_WINDOW, C_BLOCK = 8, 2, 64, 128, 128
REL_BUCKETS, REL_MAX_DIST = 32, 128
X_HEADS, X_DH = 4, 256

LANES = 128
VMEM_LIMIT = 56 * 1024 * 1024

HGRN_CHUNK = 32
HGRN_BLOCK = 512

NT_DIMS = (((1,), (1,)), ((), ()))
TN_DIMS = (((0,), (0,)), ((), ()))


def _params(*sem):
    return pltpu.CompilerParams(dimension_semantics=sem, vmem_limit_bytes=VMEM_LIMIT)


def _rms(x, g):
    return x * lax.rsqrt(jnp.mean(x * x, axis=-1, keepdims=True) + EPS) * g


def _rms_matmul_kernel(x_ref, g_ref, w_ref, o_ref, h_ref):
    @pl.when(pl.program_id(1) == 0)
    def _():
        h_ref[...] = _rms(x_ref[...], g_ref[...]).astype(BF16)

    o_ref[...] = jnp.dot(h_ref[...], w_ref[...], preferred_element_type=F32).astype(o_ref.dtype)


def _rms_matmul(x, g, w, out_dtype, tm, tn):
    T, D = x.shape
    N = w.shape[1]
    return pl.pallas_call(
        _rms_matmul_kernel,
        out_shape=jax.ShapeDtypeStruct((T, N), out_dtype),
        grid=(T // tm, N // tn),
        in_specs=[pl.BlockSpec((tm, D), lambda i, j: (i, 0)),
                  pl.BlockSpec((1, D), lambda i, j: (0, 0)),
                  pl.BlockSpec((D, tn), lambda i, j: (0, j))],
        out_specs=pl.BlockSpec((tm, tn), lambda i, j: (i, j)),
        scratch_shapes=[pltpu.VMEM((tm, D), BF16)],
        compiler_params=_params("parallel", "arbitrary"),
        name="rms_matmul",
    )(x, g.reshape(1, D), w)


def _mla_prep_kernel(za_ref, gq_ref, gkv_ref, wuq_ref, wukv_ref, cos_ref, sin_ref,
                     q_ref, kn_ref, v_ref, kr_ref, *, q_scale):
    za = za_ref[...]
    cq = za[:, :A_Q_RANK]
    ckv = za[:, A_Q_RANK:A_Q_RANK + A_KV_RANK]
    kr1 = za[:, 640:768]
    kr2 = za[:, 768:896]
    cos = cos_ref[...]
    sin = sin_ref[...]
    nope = A_HEADS * A_NOPE

    qf = jnp.dot(_rms(cq, gq_ref[...]).astype(BF16), wuq_ref[...], preferred_element_type=F32)
    q1 = qf[:, nope:nope + LANES]
    q2 = qf[:, nope + LANES:]
    q_ref[:, :nope] = (qf[:, :nope] * q_scale).astype(BF16)
    q_ref[:, nope:nope + LANES] = ((q1 * cos - q2 * sin) * q_scale).astype(BF16)
    q_ref[:, nope + LANES:] = ((q1 * sin + q2 * cos) * q_scale).astype(BF16)

    kvf = jnp.dot(_rms(ckv, gkv_ref[...]).astype(BF16), wukv_ref[...], preferred_element_type=F32)
    kn_ref[...] = kvf[:, :nope].astype(BF16)
    v_ref[...] = kvf[:, nope:].astype(BF16)
    kr_ref[:, :LANES] = (kr1 * cos - kr2 * sin).astype(BF16)
    kr_ref[:, LANES:] = (kr1 * sin + kr2 * cos).astype(BF16)


def _mla_prep(za, gq, gkv, wuq, wukv, cos, sin, seq, tm):
    T = za.shape[0]
    nope = A_HEADS * A_NOPE
    nseq = seq // tm
    q_scale = (A_NOPE + A_ROPE) ** -0.5 * math.log2(math.e)
    row = lambda i: (i, 0)
    const = lambda i: (0, 0)
    pos = lambda i: (i % nseq, 0)
    return pl.pallas_call(
        functools.partial(_mla_prep_kernel, q_scale=q_scale),
        out_shape=(jax.ShapeDtypeStruct((T, nope + 2 * LANES), BF16),
                   jax.ShapeDtypeStruct((T, nope), BF16),
                   jax.ShapeDtypeStruct((T, A_HEADS * A_V), BF16),
                   jax.ShapeDtypeStruct((T, 2 * LANES), BF16)),
        grid=(T // tm,),
        in_specs=[pl.BlockSpec((tm, 896), row),
                  pl.BlockSpec((1, A_Q_RANK), const),
                  pl.BlockSpec((1, A_KV_RANK), const),
                  pl.BlockSpec(wuq.shape, const),
                  pl.BlockSpec(wukv.shape, const),
                  pl.BlockSpec((tm, LANES), pos),
                  pl.BlockSpec((tm, LANES), pos)],
        out_specs=(pl.BlockSpec((tm, nope + 2 * LANES), row),
                   pl.BlockSpec((tm, nope), row),
                   pl.BlockSpec((tm, A_HEADS * A_V), row),
                   pl.BlockSpec((tm, 2 * LANES), row)),
        compiler_params=_params("parallel"),
        name="mla_prep",
    )(za, gq.reshape(1, -1), gkv.reshape(1, -1), wuq, wukv, cos, sin)


def _mla_attn_kernel(q_ref, k_ref, v_ref, o_ref, *, tk):
    q = q_ref[0, 0]
    tq = q.shape[0]
    nk = k_ref.shape[2] // tk

    def body(i, carry):
        m, acc = carry
        off = pl.multiple_of(i * tk, tk)
        k = k_ref[0, 0, pl.ds(off, tk), :]
        v = v_ref[0, 0, pl.ds(off, tk), :]
        s = lax.dot_general(q, k, NT_DIMS, preferred_element_type=F32)
        m_new = jnp.maximum(m, jnp.max(s, axis=-1, keepdims=True))
        alpha = jnp.exp2(m - m_new)
        p = jnp.exp2(s - m_new).astype(BF16)
        acc = alpha * acc + jnp.dot(p, v, preferred_element_type=F32)
        return m_new, acc

    m0 = jnp.full((tq, 1), MASK_VALUE, F32)
    acc0 = jnp.zeros((tq, LANES), F32)
    _, acc = lax.fori_loop(0, nk, body, (m0, acc0))
    o_ref[0, 0] = (acc[:, :A_V] / acc[:, A_V:A_V + 1]).astype(o_ref.dtype)


def _mla_attn(q, k, v, tq, tk):
    Bsz, H, S, dqk = q.shape
    return pl.pallas_call(
        functools.partial(_mla_attn_kernel, tk=tk),
        out_shape=jax.ShapeDtypeStruct((Bsz, H, S, A_V), BF16),
        grid=(Bsz, H, S // tq),
        in_specs=[pl.BlockSpec((1, 1, tq, dqk), lambda b, h, i: (b, h, i, 0)),
                  pl.BlockSpec((1, 1, S, dqk), lambda b, h, i: (b, h, 0, 0)),
                  pl.BlockSpec((1, 1, S, LANES), lambda b, h, i: (b, h, 0, 0))],
        out_specs=pl.BlockSpec((1, 1, tq, A_V), lambda b, h, i: (b, h, i, 0)),
        compiler_params=_params("parallel", "parallel", "arbitrary"),
        name="mla_attn",
    )(q, k, v)


def _chunk_cumsum(x, chunk, reverse):
    n = x.shape[0]
    row = lax.broadcasted_iota(jnp.int32, x.shape, 0) % chunk
    d = 1
    while d < chunk:
        if reverse:
            x = x + jnp.where(row < chunk - d, pltpu.roll(x, n - d, 0), 0.0)
        else:
            x = x + jnp.where(row >= d, pltpu.roll(x, d, 0), 0.0)
        d *= 2
    return x


def _hgrn_kernel(*refs, reverse, finalize):
    if finalize:
        (q_ref, z_ref, i_ref, lb_ref, prev_ref, og_ref, gout_ref, o_ref, st_ref, sall_ref) = refs
    else:
        (q_ref, z_ref, i_ref, lb_ref, o_ref, st_ref, sall_ref) = refs
    C = HGRN_CHUNK
    tb = q_ref.shape[0]
    nc = tb // C
    half = C // 2

    @pl.when(pl.program_id(2) == 0)
    def _():
        st_ref[...] = jnp.zeros_like(st_ref)

    q = q_ref[...]
    z = z_ref[...]
    lb = lb_ref[...]
    v = i_ref[...]
    f = lb + (1.0 - lb) * jax.nn.sigmoid(z)
    key = (1.0 - lb) * jax.nn.sigmoid(-z)
    b = _chunk_cumsum(jnp.log(jnp.maximum(f, TINY)), C, reverse)

    b3 = b.reshape(nc, C, 2 * B_DK)
    if reverse:
        ref3 = b3[:, half:half + 1, :]
        tot3 = b3[:, 0:1, :]
    else:
        ref3 = b3[:, half - 1:half, :]
        tot3 = b3[:, C - 1:C, :]
    qs = q.reshape(nc, C, 2 * B_DK) * jnp.exp(b3 - ref3)
    ks = key.reshape(nc, C, 2 * B_DK) * jnp.exp(ref3 - b3)
    q_dec = (qs * jnp.exp(ref3)).astype(BF16)
    k_end = (ks * jnp.exp(tot3 - ref3)).astype(BF16)
    dec = jnp.exp(tot3)
    qs = qs.astype(BF16)
    ks = ks.astype(BF16)
    v3 = v.astype(BF16).reshape(nc, C, 2 * B_DV)

    t_idx = lax.broadcasted_iota(jnp.int32, (nc, C, C), 1)
    s_idx = lax.broadcasted_iota(jnp.int32, (nc, C, C), 2)
    causal = (s_idx >= t_idx) if reverse else (s_idx <= t_idx)

    outs = []
    for h in range(2):
        sl = slice(h * B_DK, (h + 1) * B_DK)
        scores = jnp.einsum('ctk,csk->cts', qs[:, :, sl], ks[:, :, sl], preferred_element_type=F32)
        scores = jnp.where(causal, scores, 0.0).astype(BF16)
        upd = jnp.einsum('csv,csk->cvk', v3, k_end[:, :, sl], preferred_element_type=F32)
        order = range(nc - 1, -1, -1) if reverse else range(nc)
        state = st_ref[h]
        for c in order:
            sall_ref[h, c] = state.astype(BF16)
            state = state * dec[c, :, sl] + upd[c]
        st_ref[h] = state
        o_h = (jnp.einsum('cts,csv->ctv', scores, v3, preferred_element_type=F32)
               + jnp.einsum('ctk,cvk->ctv', q_dec[:, :, sl], sall_ref[h], preferred_element_type=F32))
        outs.append(o_h.reshape(tb, 2 * B_DV))
    lane = lax.broadcasted_iota(jnp.int32, (tb, 2 * B_DV), 1)
    first = lane < B_DV
    o = jnp.where(first, outs[0], outs[1])
    if not finalize:
        o_ref[...] = o
        return
    o = o + prev_ref[...]
    sq = o * o
    ms0 = jnp.sum(jnp.where(first, sq, 0.0), axis=-1, keepdims=True)
    ms1 = jnp.sum(jnp.where(first, 0.0, sq), axis=-1, keepdims=True)
    ms = jnp.where(first, ms0, ms1) * (1.0 / B_DV)
    y = o * lax.rsqrt(ms + EPS) * gout_ref[...]
    o_ref[...] = (y * jax.nn.silu(og_ref[...])).astype(o_ref.dtype)


def _hgrn_direction(zb, lb, prev, gout, seq, reverse):
    T = zb.shape[0]
    Bsz = T // seq
    tb = HGRN_BLOCK
    nblk = seq // tb
    pairs = B_HEADS // 2
    dk2, dv2 = 2 * B_DK, 2 * B_DV
    finalize = prev is not None

    def row(b, p, i):
        return b * nblk + (nblk - 1 - i if reverse else i)

    z_base = (2 if reverse else 1) * (B_HEADS * B_DK) // dk2
    i_base = 3 * (B_HEADS * B_DK) // dv2
    g_base = i_base + (B_HEADS * B_DV) // dv2
    in_specs = [pl.BlockSpec((tb, dk2), lambda b, p, i: (row(b, p, i), p)),
                pl.BlockSpec((tb, dk2), lambda b, p, i: (row(b, p, i), z_base + p)),
                pl.BlockSpec((tb, dv2), lambda b, p, i: (row(b, p, i), i_base + p)),
                pl.BlockSpec((1, dk2), lambda b, p, i: (0, p))]
    args = [zb, zb, zb, lb.reshape(1, -1)]
    if finalize:
        in_specs += [pl.BlockSpec((tb, dv2), lambda b, p, i: (row(b, p, i), p)),
                     pl.BlockSpec((tb, dv2), lambda b, p, i: (row(b, p, i), g_base + p)),
                     pl.BlockSpec((1, dv2), lambda b, p, i: (0, 0))]
        args += [prev, zb, jnp.tile(gout, 2).reshape(1, dv2)]
    return pl.pallas_call(
        functools.partial(_hgrn_kernel, reverse=reverse, finalize=finalize),
        out_shape=jax.ShapeDtypeStruct((T, B_HEADS * B_DV), BF16 if finalize else F32),
        grid=(Bsz, pairs, nblk),
        in_specs=in_specs,
        out_specs=pl.BlockSpec((tb, dv2), lambda b, p, i: (row(b, p, i), p)),
        scratch_shapes=[pltpu.VMEM((2, dv2, B_DK), F32),
                        pltpu.VMEM((2, tb // HGRN_CHUNK, dv2, B_DK), BF16)],
        compiler_params=_params("parallel", "parallel", "arbitrary"),
        name="hgrn_bwd" if reverse else "hgrn_fwd",
    )(*args)


def _t5_bucket(rel):
    nb = REL_BUCKETS // 2
    max_exact = nb // 2
    ret = (rel > 0).astype(jnp.int32) * nb
    n = jnp.abs(rel)
    large = max_exact + (jnp.log(jnp.maximum(n, 1).astype(F32) / max_exact)
                         / math.log(REL_MAX_DIST / max_exact) * (nb - max_exact)).astype(jnp.int32)
    large = jnp.minimum(large, nb - 1)
    return ret + jnp.where(n < max_exact, n, large)


def _win_kernel(sink_ref, q_ref, kp_ref, kc_ref, kn_ref, vp_ref, vc_ref, vn_ref, bias_ref, o_ref):
    G = C_HEADS // C_KV_HEADS
    blk = C_BLOCK
    kvh = pl.program_id(0)
    n = pl.program_id(2)
    last = pl.num_programs(2) - 1
    rows, span = G * blk, 3 * blk

    q = q_ref[0, 0].reshape(rows, C_DH)
    k = jnp.concatenate([kp_ref[0, 0], kc_ref[0, 0], kn_ref[0, 0]], axis=0)
    v = jnp.concatenate([vp_ref[0, 0], vc_ref[0, 0], vn_ref[0, 0]], axis=0)
    s = lax.dot_general(q, k, NT_DIMS, preferred_element_type=F32) * (C_DH ** -0.5)
    s = s + bias_ref[0].reshape(rows, span)

    r_idx = lax.broadcasted_iota(jnp.int32, (rows, span), 0) % blk
    c_idx = lax.broadcasted_iota(jnp.int32, (rows, span), 1)
    rel = c_idx - blk - r_idx
    lo = jnp.where(n == 0, blk, 0)
    hi = jnp.where(n == last, 2 * blk, span)
    valid = (jnp.abs(rel) <= C_WINDOW) & (c_idx >= lo) & (c_idx < hi)
    s = jnp.where(valid, s, MASK_VALUE)

    g_idx = lax.broadcasted_iota(jnp.int32, (rows, 1), 0) // blk
    sink = jnp.zeros((rows, 1), F32)
    for g in range(G):
        sink = jnp.where(g_idx == g, sink_ref[kvh * G + g], sink)
    m = jnp.maximum(jnp.max(s, axis=-1, keepdims=True), sink)
    p = jnp.exp(s - m)
    p = p / (jnp.sum(p, axis=-1, keepdims=True) + jnp.exp(sink - m))
    o = jnp.dot(p.astype(BF16), v, preferred_element_type=F32)
    o_ref[0, 0] = o.reshape(G, blk, C_DH).astype(o_ref.dtype)


def _window_gqa(q, k, v, bias, sink):
    Bsz, KV, G, S, dh = q.shape
    blk = C_BLOCK
    nb = S // blk
    kv_spec = lambda shift: pl.BlockSpec(
        (1, 1, blk, dh), lambda h, b, n, sref: (b, h, jnp.clip(n + shift, 0, nb - 1), 0))
    grid_spec = pltpu.PrefetchScalarGridSpec(
        num_scalar_prefetch=1,
        grid=(KV, Bsz, nb),
        in_specs=[pl.BlockSpec((1, 1, G, blk, dh), lambda h, b, n, sref: (b, h, 0, n, 0)),
                  kv_spec(-1), kv_spec(0), kv_spec(1),
                  kv_spec(-1), kv_spec(0), kv_spec(1),
                  pl.BlockSpec((1, G, blk, 3 * blk), lambda h, b, n, sref: (h, 0, 0, 0))],
        out_specs=pl.BlockSpec((1, 1, G, blk, dh), lambda h, b, n, sref: (b, h, 0, n, 0)))
    return pl.pallas_call(
        _win_kernel,
        out_shape=jax.ShapeDtypeStruct((Bsz, KV, G, S, dh), BF16),
        grid_spec=grid_spec,
        compiler_params=_params("parallel", "parallel", "arbitrary"),
        name="window_gqa",
    )(sink, q, k, k, k, v, v, v, bias)


def _merge_kernel(x_ref, ya_ref, yb_ref, yc_ref, ga_ref, gb_ref, gc_ref,
                  wa_ref, wb_ref, wc_ref, wo_ref, o_ref):
    def branch(y_ref, gate_ref, w_ref):
        return jax.nn.sigmoid(gate_ref[...]) * jnp.dot(y_ref[...], w_ref[...], preferred_element_type=F32)

    merged = (branch(ya_ref, ga_ref, wa_ref) + branch(yb_ref, gb_ref, wb_ref)
              + branch(yc_ref, gc_ref, wc_ref))
    o_ref[...] = x_ref[...] + jnp.dot(merged.astype(BF16), wo_ref[...], preferred_element_type=F32)


def _merge(x, ya, yb, yc, gates, wa, wb, wc, wo, tm):
    T, D = x.shape
    row = lambda i: (i, 0)
    const = lambda i: (0, 0)
    return pl.pallas_call(
        _merge_kernel,
        out_shape=jax.ShapeDtypeStruct((T, D), F32),
        grid=(T // tm,),
        in_specs=[pl.BlockSpec((tm, D), row),
                  pl.BlockSpec((tm, ya.shape[1]), row),
                  pl.BlockSpec((tm, yb.shape[1]), row),
                  pl.BlockSpec((tm, yc.shape[1]), row),
                  pl.BlockSpec((tm, D), lambda i: (i, 0)),
                  pl.BlockSpec((tm, D), lambda i: (i, 1)),
                  pl.BlockSpec((tm, D), lambda i: (i, 2)),
                  pl.BlockSpec(wa.shape, const),
                  pl.BlockSpec(wb.shape, const),
                  pl.BlockSpec(wc.shape, const),
                  pl.BlockSpec(wo.shape, const)],
        out_specs=pl.BlockSpec((tm, D), row),
        compiler_params=_params("parallel"),
        name="merge",
    )(x, ya, yb, yc, gates, gates, gates, wa, wb, wc, wo)


def _cross_kernel(x_ref, g_ref, kv_ref, wq_ref, wo_ref, o_ref):
    x = x_ref[...]
    h = _rms(x, g_ref[...]).astype(BF16)
    q = jnp.dot(h, wq_ref[...], preferred_element_type=F32).astype(BF16)
    kv = kv_ref[0]
    width = X_HEADS * X_DH
    outs = []
    for hd in range(X_HEADS):
        sl = slice(hd * X_DH, (hd + 1) * X_DH)
        k = kv[:, sl]
        v = kv[:, width + hd * X_DH: width + (hd + 1) * X_DH]
        s = lax.dot_general(q[:, sl], k, NT_DIMS, preferred_element_type=F32) * (X_DH ** -0.5)
        p = jnp.exp(s - jnp.max(s, axis=-1, keepdims=True))
        p = p / jnp.sum(p, axis=-1, keepdims=True)
        outs.append(jnp.dot(p.astype(BF16), v, preferred_element_type=F32).astype(BF16))
    o = jnp.concatenate(outs, axis=-1)
    o_ref[...] = x + jnp.dot(o, wo_ref[...], preferred_element_type=F32)


def _cross(x, g, kv, wq, wo, seq, tm):
    T, D = x.shape
    nseq = seq // tm
    row = lambda i: (i, 0)
    const = lambda i: (0, 0)
    return pl.pallas_call(
        _cross_kernel,
        out_shape=jax.ShapeDtypeStruct((T, D), F32),
        grid=(T // tm,),
        in_specs=[pl.BlockSpec((tm, D), row),
                  pl.BlockSpec((1, D), const),
                  pl.BlockSpec((1,) + kv.shape[1:], lambda i: (i // nseq, 0, 0)),
                  pl.BlockSpec(wq.shape, const),
                  pl.BlockSpec(wo.shape, const)],
        out_specs=pl.BlockSpec((tm, D), row),
        compiler_params=_params("parallel"),
        name="cross_attn",
    )(x, g.reshape(1, D), kv, wq, wo)


def _swiglu_kernel(x_ref, g_ref, w1_ref, w3_ref, w2_ref, gf_ref, o_ref, *, final_norm):
    x = x_ref[...]
    h = _rms(x, g_ref[...]).astype(BF16)
    a = jnp.dot(h, w1_ref[...], preferred_element_type=F32)
    b = jnp.dot(h, w3_ref[...], preferred_element_type=F32)
    u = (jax.nn.silu(a) * b).astype(BF16)
    y = x + jnp.dot(u, w2_ref[...], preferred_element_type=F32)
    if final_norm:
        y = _rms(y, gf_ref[...])
    o_ref[...] = y


def _swiglu(x, g, w1, w3, w2, g_final, final_norm, tm):
    T, D = x.shape
    row = lambda i: (i, 0)
    const = lambda i: (0, 0)
    resident = lambda shape: pl.BlockSpec(shape, const, pipeline_mode=pl.Buffered(1))
    return pl.pallas_call(
        functools.partial(_swiglu_kernel, final_norm=final_norm),
        out_shape=jax.ShapeDtypeStruct((T, D), F32),
        grid=(T // tm,),
        in_specs=[pl.BlockSpec((tm, D), row),
                  pl.BlockSpec((1, D), const),
                  resident(w1.shape), resident(w3.shape), resident(w2.shape),
                  pl.BlockSpec((1, D), const)],
        out_specs=pl.BlockSpec((tm, D), row),
        compiler_params=_params("parallel"),
        name="swiglu",
    )(x, g.reshape(1, D), w1, w3, w2, g_final.reshape(1, D))


def _pad_cols(w, width):
    return jnp.pad(w, ((0, 0), (0, width - w.shape[1])))


def _split_in_proj(w):
    o = 0
    parts = []
    for n in (A_Q_RANK, A_KV_RANK, A_ROPE, 3 * B_HEADS * B_DK + 2 * B_HEADS * B_DV,
              (C_HEADS + 2 * C_KV_HEADS) * C_DH, 3 * w.shape[0]):
        parts.append(w[:, o:o + n])
        o += n
    cq, ckv, kr, wb, wc, wg = parts
    half = A_ROPE // 2
    wa = jnp.concatenate([cq, ckv, _pad_cols(kr[:, :half], LANES), _pad_cols(kr[:, half:], LANES)], axis=1)
    return [t.astype(BF16) for t in (wa, wb, wc, wg)]


def _perm_wuq(w):
    w = w.reshape(A_Q_RANK, A_HEADS, A_NOPE + A_ROPE)
    half = A_ROPE // 2
    return jnp.concatenate([w[:, :, :A_NOPE].reshape(A_Q_RANK, -1),
                            w[:, :, A_NOPE:A_NOPE + half].reshape(A_Q_RANK, -1),
                            w[:, :, A_NOPE + half:].reshape(A_Q_RANK, -1)], axis=1).astype(BF16)


def _perm_wukv(w):
    w = w.reshape(A_KV_RANK, A_HEADS, A_NOPE + A_V)
    return jnp.concatenate([w[:, :, :A_NOPE].reshape(A_KV_RANK, -1),
                            w[:, :, A_NOPE:].reshape(A_KV_RANK, -1)], axis=1).astype(BF16)


def kernel(x, mem, w_in, g_mix, a_gq, a_gkv, a_wuq, a_wukv, b_lb, b_gout, c_sink, rel_bias,
           w_br_a, w_br_b, w_br_c, w_out, g_x, g_mem, x_wq, x_wkv, x_wo, g_ffn,
           f_w1, f_w3, f_w2, g_final):
    Bsz, S, D = x.shape
    depth = w_in.shape[0]
    T = Bsz * S
    M = mem.shape[1]
    tm = 512
    half = A_ROPE // 2

    inv = ROPE_THETA ** (-jnp.arange(half, dtype=F32) / half)
    ang = jnp.arange(S, dtype=jnp.int32).astype(F32)[:, None] * inv[None, :]
    cos = jnp.tile(jnp.cos(ang), (1, LANES // half))
    sin = jnp.tile(jnp.sin(ang), (1, LANES // half))

    sm = jax.nn.softmax(b_lb.astype(F32), axis=1)
    lower = jnp.cumsum(sm, axis=1) - sm[:, :1]

    span = 3 * C_BLOCK
    rel = jnp.arange(span)[None, :] - C_BLOCK - jnp.arange(C_BLOCK)[:, None]
    G = C_HEADS // C_KV_HEADS
    bias = jnp.transpose(rel_bias.astype(F32)[_t5_bucket(rel)], (2, 0, 1)).reshape(C_KV_HEADS, G, C_BLOCK, span)

    xt = x.reshape(T, D)
    mem2 = mem.reshape(Bsz * M, D)
    for l in range(depth):
        wa, wb, wc, wg = _split_in_proj(w_in[l])
        za = _rms_matmul(xt, g_mix[l], wa, F32, tm, wa.shape[1])
        zb = _rms_matmul(xt, g_mix[l], wb, F32, tm, 512)
        zc = _rms_matmul(xt, g_mix[l], wc, BF16, tm, wc.shape[1])
        zg = _rms_matmul(xt, g_mix[l], wg, F32, tm, 512)

        qa, kn, va, kr = _mla_prep(za, a_gq[l], a_gkv[l], _perm_wuq(a_wuq[l]), _perm_wukv(a_wukv[l]),
                                   cos, sin, S, tm)
        nope = A_HEADS * A_NOPE
        q4 = jnp.concatenate([qa[:, :nope].reshape(Bsz, S, A_HEADS, A_NOPE),
                              qa[:, nope:nope + LANES].reshape(Bsz, S, A_HEADS, half),
                              qa[:, nope + LANES:].reshape(Bsz, S, A_HEADS, half)], axis=-1)
        kr4 = jnp.concatenate([kr[:, :half], kr[:, LANES:LANES + half]], axis=-1).reshape(Bsz, S, 1, A_ROPE)
        k4 = jnp.concatenate([kn.reshape(Bsz, S, A_HEADS, A_NOPE),
                              jnp.broadcast_to(kr4, (Bsz, S, A_HEADS, A_ROPE))], axis=-1)
        v4 = jnp.concatenate([va.reshape(Bsz, S, A_HEADS, A_V),
                              jnp.ones((Bsz, S, A_HEADS, 1), BF16),
                              jnp.zeros((Bsz, S, A_HEADS, LANES - A_V - 1), BF16)], axis=-1)
        tr = lambda t: jnp.transpose(t, (0, 2, 1, 3))
        oa = _mla_attn(tr(q4), tr(k4), tr(v4), 1024, 1024)
        ya = tr(oa).reshape(T, A_HEADS * A_V)

        of = _hgrn_direction(zb, lower[0, l], None, None, S, reverse=False)
        yb = _hgrn_direction(zb, lower[1, l], of, b_gout[l], S, reverse=True)

        qw = C_HEADS * C_DH
        kw = C_KV_HEADS * C_DH
        cq = jnp.transpose(zc[:, :qw].reshape(Bsz, S, C_KV_HEADS, G, C_DH), (0, 2, 3, 1, 4))
        ck = jnp.transpose(zc[:, qw:qw + kw].reshape(Bsz, S, C_KV_HEADS, C_DH), (0, 2, 1, 3))
        cv = jnp.transpose(zc[:, qw + kw:].reshape(Bsz, S, C_KV_HEADS, C_DH), (0, 2, 1, 3))
        oc = _window_gqa(cq, ck, cv, bias, c_sink[l].astype(F32))
        yc = jnp.transpose(oc, (0, 3, 1, 2, 4)).reshape(T, qw)

        xt = _merge(xt, ya, yb, yc, zg, w_br_a[l].astype(BF16), w_br_b[l].astype(BF16),
                    w_br_c[l].astype(BF16), w_out[l].astype(BF16), tm)

        kvm = _rms_matmul(mem2, g_mem[l], x_wkv[l].astype(BF16), BF16, Bsz * M, 512)
        xt = _cross(xt, g_x[l], kvm.reshape(Bsz, M, -1), x_wq[l].astype(BF16), x_wo[l].astype(BF16), S, tm)

        xt = _swiglu(xt, g_ffn[l], f_w1[l].astype(BF16), f_w3[l].astype(BF16), f_w2[l].astype(BF16),
                     g_final, l == depth - 1, tm)
    return xt.reshape(Bsz, S, D)
```

```python
import functools
import math

import jax
import jax.numpy as jnp
from jax import lax
from jax.experimental import pallas as pl
from jax.experimental.pallas import tpu as pltpu

F32 = jnp.float32
BF16 = jnp.bfloat16

EPS = 1e-6
MASK_VALUE = -1e30
TINY = 1e-30
ROPE_THETA = 10000.0

A_HEADS, A_NOPE, A_ROPE, A_V = 8, 64, 32, 64
A_Q_RANK, A_KV_RANK = 384, 256
B_HEADS, B_DK, B_DV = 8, 128, 64
C_HEADS, C_KV_HEADS, C_DH, C_WINDOW, C_BLOCK = 8, 2, 64, 128, 128
REL_BUCKETS, REL_MAX_DIST = 32, 128
X_HEADS, X_DH = 4, 256

LANES = 128
VMEM_LIMIT = 56 * 1024 * 1024

MLA_VROWS = 80
MLA_STEPS = 2
MLA_SLAB = 16
HGRN_CHUNK = 32
HGRN_BLOCK = 512

NT_DIMS = (((1,), (1,)), ((), ()))
TN_DIMS = (((0,), (0,)), ((), ()))


def _params(*sem):
    return pltpu.CompilerParams(dimension_semantics=sem, vmem_limit_bytes=VMEM_LIMIT)


def _rms(x, g):
    return x * lax.rsqrt(jnp.mean(x * x, axis=-1, keepdims=True) + EPS) * g


def _rms_matmul_kernel(x_ref, g_ref, w_ref, o_ref, h_ref):
    @pl.when(pl.program_id(1) == 0)
    def _():
        h_ref[...] = _rms(x_ref[...], g_ref[...]).astype(BF16)

    o_ref[...] = jnp.dot(h_ref[...], w_ref[...], preferred_element_type=F32).astype(o_ref.dtype)


def _rms_matmul(x, g, w, out_dtype, tm, tn):
    T, D = x.shape
    N = w.shape[1]
    return pl.pallas_call(
        _rms_matmul_kernel,
        out_shape=jax.ShapeDtypeStruct((T, N), out_dtype),
        grid=(T // tm, N // tn),
        in_specs=[pl.BlockSpec((tm, D), lambda i, j: (i, 0)),
                  pl.BlockSpec((1, D), lambda i, j: (0, 0)),
                  pl.BlockSpec((D, tn), lambda i, j: (0, j))],
        out_specs=pl.BlockSpec((tm, tn), lambda i, j: (i, j)),
        scratch_shapes=[pltpu.VMEM((tm, D), BF16)],
        compiler_params=_params("parallel", "arbitrary"),
        name="rms_matmul",
    )(x, g.reshape(1, D), w)


def _mla_prep_kernel(za_ref, gq_ref, gkv_ref, wuq_ref, wukv_ref, cos_ref, sin_ref,
                     q_ref, kn_ref, v_ref, kr_ref, *, q_scale):
    za = za_ref[...]
    cq = za[:, :A_Q_RANK]
    ckv = za[:, A_Q_RANK:A_Q_RANK + A_KV_RANK]
    kr1 = za[:, 640:768]
    kr2 = za[:, 768:896]
    cos = cos_ref[...]
    sin = sin_ref[...]
    nope = A_HEADS * A_NOPE

    qf = jnp.dot(_rms(cq, gq_ref[...]).astype(BF16), wuq_ref[...], preferred_element_type=F32)
    q1 = qf[:, nope:nope + LANES]
    q2 = qf[:, nope + LANES:]
    q_ref[:, :nope] = (qf[:, :nope] * q_scale).astype(BF16)
    q_ref[:, nope:nope + LANES] = ((q1 * cos - q2 * sin) * q_scale).astype(BF16)
    q_ref[:, nope + LANES:] = ((q1 * sin + q2 * cos) * q_scale).astype(BF16)

    kvf = jnp.dot(_rms(ckv, gkv_ref[...]).astype(BF16), wukv_ref[...], preferred_element_type=F32)
    kn_ref[...] = kvf[:, :nope].astype(BF16)
    v_ref[...] = kvf[:, nope:].astype(BF16)
    kr_ref[:, :LANES] = (kr1 * cos - kr2 * sin).astype(BF16)
    kr_ref[:, LANES:] = (kr1 * sin + kr2 * cos).astype(BF16)


def _mla_prep(za, gq, gkv, wuq, wukv, cos, sin, seq, tm):
    T = za.shape[0]
    nope = A_HEADS * A_NOPE
    nseq = seq // tm
    q_scale = (A_NOPE + A_ROPE) ** -0.5 * math.log2(math.e)
    row = lambda i: (i, 0)
    const = lambda i: (0, 0)
    pos = lambda i: (i % nseq, 0)
    return pl.pallas_call(
        functools.partial(_mla_prep_kernel, q_scale=q_scale),
        out_shape=(jax.ShapeDtypeStruct((T, nope + 2 * LANES), BF16),
                   jax.ShapeDtypeStruct((T, nope), BF16),
                   jax.ShapeDtypeStruct((T, A_HEADS * A_V), BF16),
                   jax.ShapeDtypeStruct((T, 2 * LANES), BF16)),
        grid=(T // tm,),
        in_specs=[pl.BlockSpec((tm, 896), row),
                  pl.BlockSpec((1, A_Q_RANK), const),
                  pl.BlockSpec((1, A_KV_RANK), const),
                  pl.BlockSpec(wuq.shape, const),
                  pl.BlockSpec(wukv.shape, const),
                  pl.BlockSpec((tm, LANES), pos),
                  pl.BlockSpec((tm, LANES), pos)],
        out_specs=(pl.BlockSpec((tm, nope + 2 * LANES), row),
                   pl.BlockSpec((tm, nope), row),
                   pl.BlockSpec((tm, A_HEADS * A_V), row),
                   pl.BlockSpec((tm, 2 * LANES), row)),
        compiler_params=_params("parallel"),
        name="mla_prep",
    )(za, gq.reshape(1, -1), gkv.reshape(1, -1), wuq, wukv, cos, sin)


def _mla_attn_kernel(qt_ref, k_ref, vt_ref, o_ref, s_buf, p_buf, *, sub):
    qt = qt_ref[0, 0]
    tq = qt.shape[1]
    n_sub = k_ref.shape[2] // sub

    def qk(j, slot):
        off = pl.multiple_of(jnp.minimum(j, n_sub - 1) * sub, sub)
        s = jnp.dot(k_ref[0, 0, pl.ds(off, sub), :], qt, preferred_element_type=F32)
        s_buf[slot] = s
        return jnp.max(s, axis=0, keepdims=True)

    def step(j, slot, m, acc, cmax):
        cmax_next = qk(j + 1, 1 - slot)
        m_new = jnp.maximum(m, cmax)
        alpha = jnp.exp2(m - m_new)
        mb = jnp.broadcast_to(m_new, (MLA_SLAB, tq))
        for r in range(sub // MLA_SLAB):
            rows = pl.ds(r * MLA_SLAB, MLA_SLAB)
            p_buf[slot, rows, :] = jnp.exp2(s_buf[slot, rows, :] - mb).astype(BF16)
        off = pl.multiple_of(j * sub, sub)
        vt = vt_ref[0, 0, :, pl.ds(off, sub)]
        acc = alpha * acc + jnp.dot(vt, p_buf[slot], preferred_element_type=F32)
        return m_new, acc, cmax_next

    def body(i, carry):
        for u in range(MLA_STEPS):
            carry = step(MLA_STEPS * i + u, u % 2, *carry)
        return carry

    m0 = jnp.full((1, tq), MASK_VALUE, F32)
    acc0 = jnp.zeros((MLA_VROWS, tq), F32)
    _, acc, _ = lax.fori_loop(0, n_sub // MLA_STEPS, body, (m0, acc0, qk(0, 0)))
    o_ref[0, 0] = (acc[:A_V] / acc[A_V:A_V + 1]).astype(o_ref.dtype)


def _mla_attn(qt, k, vt, tq, sub):
    Bsz, H, dqk, S = qt.shape
    return pl.pallas_call(
        functools.partial(_mla_attn_kernel, sub=sub),
        out_shape=jax.ShapeDtypeStruct((Bsz, H, A_V, S), BF16),
        grid=(Bsz, H, S // tq),
        in_specs=[pl.BlockSpec((1, 1, dqk, tq), lambda b, h, i: (b, h, 0, i)),
                  pl.BlockSpec((1, 1, S, dqk), lambda b, h, i: (b, h, 0, 0)),
                  pl.BlockSpec((1, 1, MLA_VROWS, S), lambda b, h, i: (b, h, 0, 0))],
        out_specs=pl.BlockSpec((1, 1, A_V, tq), lambda b, h, i: (b, h, 0, i)),
        scratch_shapes=[pltpu.VMEM((2, sub, tq), F32), pltpu.VMEM((2, sub, tq), BF16)],
        compiler_params=_params("parallel", "parallel", "arbitrary"),
        name="mla_attn",
    )(qt, k, vt)


def _chunk_cumsum(x, chunk, reverse):
    n = x.shape[0]
    row = lax.broadcasted_iota(jnp.int32, x.shape, 0) % chunk
    d = 1
    while d < chunk:
        if reverse:
            x = x + jnp.where(row < chunk - d, pltpu.roll(x, n - d, 0), 0.0)
        else:
            x = x + jnp.where(row >= d, pltpu.roll(x, d, 0), 0.0)
        d *= 2
    return x


def _hgrn_kernel(*refs, reverse, finalize):
    if finalize:
        (q_ref, z_ref, i_ref, lb_ref, prev_ref, og_ref, gout_ref, o_ref, st_ref, sall_ref) = refs
    else:
        (q_ref, z_ref, i_ref, lb_ref, o_ref, st_ref, sall_ref) = refs
    C = HGRN_CHUNK
    tb = q_ref.shape[0]
    nc = tb // C
    half = C // 2

    @pl.when(pl.program_id(2) == 0)
    def _():
        st_ref[...] = jnp.zeros_like(st_ref)

    q = q_ref[...]
    z = z_ref[...]
    lb = lb_ref[...]
    v = i_ref[...]
    f = lb + (1.0 - lb) * jax.nn.sigmoid(z)
    key = (1.0 - lb) * jax.nn.sigmoid(-z)
    b = _chunk_cumsum(jnp.log(jnp.maximum(f, TINY)), C, reverse)

    b3 = b.reshape(nc, C, 2 * B_DK)
    if reverse:
        ref3 = b3[:, half:half + 1, :]
        tot3 = b3[:, 0:1, :]
    else:
        ref3 = b3[:, half - 1:half, :]
        tot3 = b3[:, C - 1:C, :]
    qs = q.reshape(nc, C, 2 * B_DK) * jnp.exp(b3 - ref3)
    ks = key.reshape(nc, C, 2 * B_DK) * jnp.exp(ref3 - b3)
    q_dec = (qs * jnp.exp(ref3)).astype(BF16)
    k_end = (ks * jnp.exp(tot3 - ref3)).astype(BF16)
    dec = jnp.exp(tot3)
    qs = qs.astype(BF16)
    ks = ks.astype(BF16)
    v3 = v.astype(BF16).reshape(nc, C, 2 * B_DV)

    t_idx = lax.broadcasted_iota(jnp.int32, (nc, C, C), 1)
    s_idx = lax.broadcasted_iota(jnp.int32, (nc, C, C), 2)
    causal = (s_idx >= t_idx) if reverse else (s_idx <= t_idx)

    outs = []
    for h in range(2):
        sl = slice(h * B_DK, (h + 1) * B_DK)
        scores = jnp.einsum('ctk,csk->cts', qs[:, :, sl], ks[:, :, sl], preferred_element_type=F32)
        scores = jnp.where(causal, scores, 0.0).astype(BF16)
        upd = jnp.einsum('csv,csk->cvk', v3, k_end[:, :, sl], preferred_element_type=F32)
        order = range(nc - 1, -1, -1) if reverse else range(nc)
        state = st_ref[h]
        for c in order:
            sall_ref[h, c] = state.astype(BF16)
            state = state * dec[c, :, sl] + upd[c]
        st_ref[h] = state
        o_h = (jnp.einsum('cts,csv->ctv', scores, v3, preferred_element_type=F32)
               + jnp.einsum('ctk,cvk->ctv', q_dec[:, :, sl], sall_ref[h], preferred_element_type=F32))
        outs.append(o_h.reshape(tb, 2 * B_DV))
    lane = lax.broadcasted_iota(jnp.int32, (tb, 2 * B_DV), 1)
    first = lane < B_DV
    o = jnp.where(first, outs[0], outs[1])
    if not finalize:
        o_ref[...] = o
        return
    o = o + prev_ref[...]
    sq = o * o
    ms0 = jnp.sum(jnp.where(first, sq, 0.0), axis=-1, keepdims=True)
    ms1 = jnp.sum(jnp.where(first, 0.0, sq), axis=-1, keepdims=True)
    ms = jnp.where(first, ms0, ms1) * (1.0 / B_DV)
    y = o * lax.rsqrt(ms + EPS) * gout_ref[...]
    o_ref[...] = (y * jax.nn.silu(og_ref[...])).astype(o_ref.dtype)


def _hgrn_direction(zb, lb, prev, gout, seq, reverse):
    T = zb.shape[0]
    Bsz = T // seq
    tb = HGRN_BLOCK
    nblk = seq // tb
    pairs = B_HEADS // 2
    dk2, dv2 = 2 * B_DK, 2 * B_DV
    finalize = prev is not None

    def row(b, p, i):
        return b * nblk + (nblk - 1 - i if reverse else i)

    z_base = (2 if reverse else 1) * (B_HEADS * B_DK) // dk2
    i_base = 3 * (B_HEADS * B_DK) // dv2
    g_base = i_base + (B_HEADS * B_DV) // dv2
    in_specs = [pl.BlockSpec((tb, dk2), lambda b, p, i: (row(b, p, i), p)),
                pl.BlockSpec((tb, dk2), lambda b, p, i: (row(b, p, i), z_base + p)),
                pl.BlockSpec((tb, dv2), lambda b, p, i: (row(b, p, i), i_base + p)),
                pl.BlockSpec((1, dk2), lambda b, p, i: (0, p))]
    args = [zb, zb, zb, lb.reshape(1, -1)]
    if finalize:
        in_specs += [pl.BlockSpec((tb, dv2), lambda b, p, i: (row(b, p, i), p)),
                     pl.BlockSpec((tb, dv2), lambda b, p, i: (row(b, p, i), g_base + p)),
                     pl.BlockSpec((1, dv2), lambda b, p, i: (0, 0))]
        args += [prev, zb, jnp.tile(gout, 2).reshape(1, dv2)]
    return pl.pallas_call(
        functools.partial(_hgrn_kernel, reverse=reverse, finalize=finalize),
        out_shape=jax.ShapeDtypeStruct((T, B_HEADS * B_DV), BF16 if finalize else F32),
        grid=(Bsz, pairs, nblk),
        in_specs=in_specs,
        out_specs=pl.BlockSpec((tb, dv2), lambda b, p, i: (row(b, p, i), p)),
        scratch_shapes=[pltpu.VMEM((2, dv2, B_DK), F32),
                        pltpu.VMEM((2, tb // HGRN_CHUNK, dv2, B_DK), BF16)],
        compiler_params=_params("parallel", "parallel", "arbitrary"),
        name="hgrn_bwd" if reverse else "hgrn_fwd",
    )(*args)


def _t5_bucket(rel):
    nb = REL_BUCKETS // 2
    max_exact = nb // 2
    ret = (rel > 0).astype(jnp.int32) * nb
    n = jnp.abs(rel)
    large = max_exact + (jnp.log(jnp.maximum(n, 1).astype(F32) / max_exact)
                         / math.log(REL_MAX_DIST / max_exact) * (nb - max_exact)).astype(jnp.int32)
    large = jnp.minimum(large, nb - 1)
    return ret + jnp.where(n < max_exact, n, large)


def _win_kernel(sink_ref, q_ref, kp_ref, kc_ref, kn_ref, vp_ref, vc_ref, vn_ref, bias_ref, o_ref):
    G = C_HEADS // C_KV_HEADS
    blk = C_BLOCK
    kvh = pl.program_id(0)
    n = pl.program_id(2)
    last = pl.num_programs(2) - 1
    rows, span = G * blk, 3 * blk

    q = q_ref[0, 0].reshape(rows, C_DH)
    k = jnp.concatenate([kp_ref[0, 0], kc_ref[0, 0], kn_ref[0, 0]], axis=0)
    v = jnp.concatenate([vp_ref[0, 0], vc_ref[0, 0], vn_ref[0, 0]], axis=0)
    s = lax.dot_general(q, k, NT_DIMS, preferred_element_type=F32) * (C_DH ** -0.5)
    s = s + bias_ref[0].reshape(rows, span)

    r_idx = lax.broadcasted_iota(jnp.int32, (rows, span), 0) % blk
    c_idx = lax.broadcasted_iota(jnp.int32, (rows, span), 1)
    rel = c_idx - blk - r_idx
    lo = jnp.where(n == 0, blk, 0)
    hi = jnp.where(n == last, 2 * blk, span)
    valid = (jnp.abs(rel) <= C_WINDOW) & (c_idx >= lo) & (c_idx < hi)
    s = jnp.where(valid, s, MASK_VALUE)

    g_idx = lax.broadcasted_iota(jnp.int32, (rows, 1), 0) // blk
    sink = jnp.zeros((rows, 1), F32)
    for g in range(G):
        sink = jnp.where(g_idx == g, sink_ref[kvh * G + g], sink)
    m = jnp.maximum(jnp.max(s, axis=-1, keepdims=True), sink)
    p = jnp.exp(s - m)
    p = p / (jnp.sum(p, axis=-1, keepdims=True) + jnp.exp(sink - m))
    o = jnp.dot(p.astype(BF16), v, preferred_element_type=F32)
    o_ref[0, 0] = o.reshape(G, blk, C_DH).astype(o_ref.dtype)


def _window_gqa(q, k, v, bias, sink):
    Bsz, KV, G, S, dh = q.shape
    blk = C_BLOCK
    nb = S // blk
    kv_spec = lambda shift: pl.BlockSpec(
        (1, 1, blk, dh), lambda h, b, n, sref: (b, h, jnp.clip(n + shift, 0, nb - 1), 0))
    grid_spec = pltpu.PrefetchScalarGridSpec(
        num_scalar_prefetch=1,
        grid=(KV, Bsz, nb),
        in_specs=[pl.BlockSpec((1, 1, G, blk, dh), lambda h, b, n, sref: (b, h, 0, n, 0)),
                  kv_spec(-1), kv_spec(0), kv_spec(1),
                  kv_spec(-1), kv_spec(0), kv_spec(1),
                  pl.BlockSpec((1, G, blk, 3 * blk), lambda h, b, n, sref: (h, 0, 0, 0))],
        out_specs=pl.BlockSpec((1, 1, G, blk, dh), lambda h, b, n, sref: (b, h, 0, n, 0)))
    return pl.pallas_call(
        _win_kernel,
        out_shape=jax.ShapeDtypeStruct((Bsz, KV, G, S, dh), BF16),
        grid_spec=grid_spec,
        compiler_params=_params("parallel", "parallel", "arbitrary"),
        name="window_gqa",
    )(sink, q, k, k, k, v, v, v, bias)


def _merge_kernel(x_ref, ya_ref, yb_ref, yc_ref, ga_ref, gb_ref, gc_ref,
                  wa_ref, wb_ref, wc_ref, wo_ref, o_ref):
    def branch(y_ref, gate_ref, w_ref):
        return jax.nn.sigmoid(gate_ref[...]) * jnp.dot(y_ref[...], w_ref[...], preferred_element_type=F32)

    merged = (branch(ya_ref, ga_ref, wa_ref) + branch(yb_ref, gb_ref, wb_ref)
              + branch(yc_ref, gc_ref, wc_ref))
    o_ref[...] = x_ref[...] + jnp.dot(merged.astype(BF16), wo_ref[...], preferred_element_type=F32)


def _merge(x, ya, yb, yc, gates, wa, wb, wc, wo, tm):
    T, D = x.shape
    row = lambda i: (i, 0)
    const = lambda i: (0, 0)
    return pl.pallas_call(
        _merge_kernel,
        out_shape=jax.ShapeDtypeStruct((T, D), F32),
        grid=(T // tm,),
        in_specs=[pl.BlockSpec((tm, D), row),
                  pl.BlockSpec((tm, ya.shape[1]), row),
                  pl.BlockSpec((tm, yb.shape[1]), row),
                  pl.BlockSpec((tm, yc.shape[1]), row),
                  pl.BlockSpec((tm, D), lambda i: (i, 0)),
                  pl.BlockSpec((tm, D), lambda i: (i, 1)),
                  pl.BlockSpec((tm, D), lambda i: (i, 2)),
                  pl.BlockSpec(wa.shape, const),
                  pl.BlockSpec(wb.shape, const),
                  pl.BlockSpec(wc.shape, const),
                  pl.BlockSpec(wo.shape, const)],
        out_specs=pl.BlockSpec((tm, D), row),
        compiler_params=_params("parallel"),
        name="merge",
    )(x, ya, yb, yc, gates, gates, gates, wa, wb, wc, wo)


def _cross_kernel(x_ref, g_ref, kv_ref, wq_ref, wo_ref, o_ref):
    x = x_ref[...]
    h = _rms(x, g_ref[...]).astype(BF16)
    q = jnp.dot(h, wq_ref[...], preferred_element_type=F32).astype(BF16)
    kv = kv_ref[0]
    width = X_HEADS * X_DH
    outs = []
    for hd in range(X_HEADS):
        sl = slice(hd * X_DH, (hd + 1) * X_DH)
        k = kv[:, sl]
        v = kv[:, width + hd * X_DH: width + (hd + 1) * X_DH]
        s = lax.dot_general(q[:, sl], k, NT_DIMS, preferred_element_type=F32) * (X_DH ** -0.5)
        p = jnp.exp(s - jnp.max(s, axis=-1, keepdims=True))
        p = p / jnp.sum(p, axis=-1, keepdims=True)
        outs.append(jnp.dot(p.astype(BF16), v, preferred_element_type=F32).astype(BF16))
    o = jnp.concatenate(outs, axis=-1)
    o_ref[...] = x + jnp.dot(o, wo_ref[...], preferred_element_type=F32)


def _cross(x, g, kv, wq, wo, seq, tm):
    T, D = x.shape
    nseq = seq // tm
    row = lambda i: (i, 0)
    const = lambda i: (0, 0)
    return pl.pallas_call(
        _cross_kernel,
        out_shape=jax.ShapeDtypeStruct((T, D), F32),
        grid=(T // tm,),
        in_specs=[pl.BlockSpec((tm, D), row),
                  pl.BlockSpec((1, D), const),
                  pl.BlockSpec((1,) + kv.shape[1:], lambda i: (i // nseq, 0, 0)),
                  pl.BlockSpec(wq.shape, const),
                  pl.BlockSpec(wo.shape, const)],
        out_specs=pl.BlockSpec((tm, D), row),
        compiler_params=_params("parallel"),
        name="cross_attn",
    )(x, g.reshape(1, D), kv, wq, wo)


def _swiglu_kernel(x_ref, g_ref, w1_ref, w3_ref, w2_ref, gf_ref, o_ref, *, final_norm):
    x = x_ref[...]
    h = _rms(x, g_ref[...]).astype(BF16)
    a = jnp.dot(h, w1_ref[...], preferred_element_type=F32)
    b = jnp.dot(h, w3_ref[...], preferred_element_type=F32)
    u = (jax.nn.silu(a) * b).astype(BF16)
    y = x + jnp.dot(u, w2_ref[...], preferred_element_type=F32)
    if final_norm:
        y = _rms(y, gf_ref[...])
    o_ref[...] = y


def _swiglu(x, g, w1, w3, w2, g_final, final_norm, tm):
    T, D = x.shape
    row = lambda i: (i, 0)
    const = lambda i: (0, 0)
    resident = lambda shape: pl.BlockSpec(shape, const, pipeline_mode=pl.Buffered(1))
    return pl.pallas_call(
        functools.partial(_swiglu_kernel, final_norm=final_norm),
        out_shape=jax.ShapeDtypeStruct((T, D), F32),
        grid=(T // tm,),
        in_specs=[pl.BlockSpec((tm, D), row),
                  pl.BlockSpec((1, D), const),
                  resident(w1.shape), resident(w3.shape), resident(w2.shape),
                  pl.BlockSpec((1, D), const)],
        out_specs=pl.BlockSpec((tm, D), row),
        compiler_params=_params("parallel"),
        name="swiglu",
    )(x, g.reshape(1, D), w1, w3, w2, g_final.reshape(1, D))


def _pad_cols(w, width):
    return jnp.pad(w, ((0, 0), (0, width - w.shape[1])))


def _split_in_proj(w):
    o = 0
    parts = []
    for n in (A_Q_RANK, A_KV_RANK, A_ROPE, 3 * B_HEADS * B_DK + 2 * B_HEADS * B_DV,
              (C_HEADS + 2 * C_KV_HEADS) * C_DH, 3 * w.shape[0]):
        parts.append(w[:, o:o + n])
        o += n
    cq, ckv, kr, wb, wc, wg = parts
    half = A_ROPE // 2
    wa = jnp.concatenate([cq, ckv, _pad_cols(kr[:, :half], LANES), _pad_cols(kr[:, half:], LANES)], axis=1)
    return [t.astype(BF16) for t in (wa, wb, wc, wg)]


def _perm_wuq(w):
    w = w.reshape(A_Q_RANK, A_HEADS, A_NOPE + A_ROPE)
    half = A_ROPE // 2
    return jnp.concatenate([w[:, :, :A_NOPE].reshape(A_Q_RANK, -1),
                            w[:, :, A_NOPE:A_NOPE + half].reshape(A_Q_RANK, -1),
                            w[:, :, A_NOPE + half:].reshape(A_Q_RANK, -1)], axis=1).astype(BF16)


def _perm_wukv(w):
    w = w.reshape(A_KV_RANK, A_HEADS, A_NOPE + A_V)
    return jnp.concatenate([w[:, :, :A_NOPE].reshape(A_KV_RANK, -1),
                            w[:, :, A_NOPE:].reshape(A_KV_RANK, -1)], axis=1).astype(BF16)


def kernel(x, mem, w_in, g_mix, a_gq, a_gkv, a_wuq, a_wukv, b_lb, b_gout, c_sink, rel_bias,
           w_br_a, w_br_b, w_br_c, w_out, g_x, g_mem, x_wq, x_wkv, x_wo, g_ffn,
           f_w1, f_w3, f_w2, g_final):
    Bsz, S, D = x.shape
    depth = w_in.shape[0]
    T = Bsz * S
    M = mem.shape[1]
    tm = 512
    half = A_ROPE // 2

    inv = ROPE_THETA ** (-jnp.arange(half, dtype=F32) / half)
    ang = jnp.arange(S, dtype=jnp.int32).astype(F32)[:, None] * inv[None, :]
    cos = jnp.tile(jnp.cos(ang), (1, LANES // half))
    sin = jnp.tile(jnp.sin(ang), (1, LANES // half))

    sm = jax.nn.softmax(b_lb.astype(F32), axis=1)
    lower = jnp.cumsum(sm, axis=1) - sm[:, :1]

    span = 3 * C_BLOCK
    rel = jnp.arange(span)[None, :] - C_BLOCK - jnp.arange(C_BLOCK)[:, None]
    G = C_HEADS // C_KV_HEADS
    bias = jnp.transpose(rel_bias.astype(F32)[_t5_bucket(rel)], (2, 0, 1)).reshape(C_KV_HEADS, G, C_BLOCK, span)

    xt = x.reshape(T, D)
    mem2 = mem.reshape(Bsz * M, D)
    for l in range(depth):
        wa, wb, wc, wg = _split_in_proj(w_in[l])
        za = _rms_matmul(xt, g_mix[l], wa, F32, tm, wa.shape[1])
        zb = _rms_matmul(xt, g_mix[l], wb, F32, tm, 512)
        zc = _rms_matmul(xt, g_mix[l], wc, BF16, tm, wc.shape[1])
        zg = _rms_matmul(xt, g_mix[l], wg, F32, tm, 512)

        qa, kn, va, kr = _mla_prep(za, a_gq[l], a_gkv[l], _perm_wuq(a_wuq[l]), _perm_wukv(a_wukv[l]),
                                   cos, sin, S, tm)
        nope = A_HEADS * A_NOPE
        q4 = jnp.concatenate([qa[:, :nope].reshape(Bsz, S, A_HEADS, A_NOPE),
                              qa[:, nope:nope + LANES].reshape(Bsz, S, A_HEADS, half),
                              qa[:, nope + LANES:].reshape(Bsz, S, A_HEADS, half)], axis=-1)
        kr4 = jnp.concatenate([kr[:, :half], kr[:, LANES:LANES + half]], axis=-1).reshape(Bsz, S, 1, A_ROPE)
        k4 = jnp.concatenate([kn.reshape(Bsz, S, A_HEADS, A_NOPE),
                              jnp.broadcast_to(kr4, (Bsz, S, A_HEADS, A_ROPE))], axis=-1)
        v4 = jnp.concatenate([va.reshape(Bsz, S, A_HEADS, A_V),
                              jnp.ones((Bsz, S, A_HEADS, 1), BF16),
                              jnp.zeros((Bsz, S, A_HEADS, MLA_VROWS - A_V - 1), BF16)], axis=-1)
        oa = _mla_attn(jnp.transpose(q4, (0, 2, 3, 1)), jnp.transpose(k4, (0, 2, 1, 3)),
                       jnp.transpose(v4, (0, 2, 3, 1)), 512, 512)
        ya = jnp.transpose(oa, (0, 3, 1, 2)).reshape(T, A_HEADS * A_V)

        of = _hgrn_direction(zb, lower[0, l], None, None, S, reverse=False)
        yb = _hgrn_direction(zb, lower[1, l], of, b_gout[l], S, reverse=True)

        qw = C_HEADS * C_DH
        kw = C_KV_HEADS * C_DH
        cq = jnp.transpose(zc[:, :qw].reshape(Bsz, S, C_KV_HEADS, G, C_DH), (0, 2, 3, 1, 4))
        ck = jnp.transpose(zc[:, qw:qw + kw].reshape(Bsz, S, C_KV_HEADS, C_DH), (0, 2, 1, 3))
        cv = jnp.transpose(zc[:, qw + kw:].reshape(Bsz, S, C_KV_HEADS, C_DH), (0, 2, 1, 3))
        oc = _window_gqa(cq, ck, cv, bias, c_sink[l].astype(F32))
        yc = jnp.transpose(oc, (0, 3, 1, 2, 4)).reshape(T, qw)

        xt = _merge(xt, ya, yb, yc, zg, w_br_a[l].astype(BF16), w_br_b[l].astype(BF16),
                    w_br_c[l].astype(BF16), w_out[l].astype(BF16), tm)

        kvm = _rms_matmul(mem2, g_mem[l], x_wkv[l].astype(BF16), BF16, Bsz * M, 512)
        xt = _cross(xt, g_x[l], kvm.reshape(Bsz, M, -1), x_wq[l].astype(BF16), x_wo[l].astype(BF16), S, tm)

        xt = _swiglu(xt, g_ffn[l], f_w1[l].astype(BF16), f_w3[l].astype(BF16), f_w2[l].astype(BF16),
                     g_final, l == depth - 1, tm)
    return xt.reshape(Bsz, S, D)
```

```python
import functools
import math

import jax
import jax.numpy as jnp
from jax import lax
from jax.experimental import pallas as pl
from jax.experimental.pallas import tpu as pltpu

F32 = jnp.float32
BF16 = jnp.bfloat16

EPS = 1e-6
MASK_VALUE = -1e30
TINY = 1e-30
ROPE_THETA = 10000.0

A_HEADS, A_NOPE, A_ROPE, A_V = 8, 64, 32, 64
A_Q_RANK, A_KV_RANK = 384, 256
B_HEADS, B_DK, B_DV = 8, 128, 64
C_HEADS, C_KV_HEADS, C_DH, C_WINDOW, C_BLOCK = 8, 2, 64, 128, 128
REL_BUCKETS, REL_MAX_DIST = 32, 128
X_HEADS, X_DH = 4, 256

LANES = 128
VMEM_LIMIT = 56 * 1024 * 1024

ROW_TILE = 512
IN_PROJ_TILE = 256
MLA_TQ = 512
MLA_SUB = 512
MLA_STEPS = 2
MLA_SLAB = 16
MLA_VROWS = 80
HGRN_CHUNK = 32
HGRN_BLOCK = 512

NT_DIMS = (((1,), (1,)), ((), ()))


def _params(*sem):
    return pltpu.CompilerParams(dimension_semantics=sem, vmem_limit_bytes=VMEM_LIMIT)


def _rms(x, g):
    return x * lax.rsqrt(jnp.mean(x * x, axis=-1, keepdims=True) + EPS) * g


def _resident(shape):
    return pl.BlockSpec(shape, lambda *_: (0,) * len(shape), pipeline_mode=pl.Buffered(1))


def _rms_proj_kernel(*refs, n_out):
    x_ref, g_ref = refs[:2]
    w_refs = refs[2:2 + n_out]
    o_refs = refs[2 + n_out:]
    h = _rms(x_ref[...], g_ref[...]).astype(BF16)
    for w_ref, o_ref in zip(w_refs, o_refs):
        o_ref[...] = jnp.dot(h, w_ref[...], preferred_element_type=F32).astype(o_ref.dtype)


def _rms_proj(x, g, weights, out_dtypes, tm):
    T, D = x.shape
    row = lambda i: (i, 0)
    return pl.pallas_call(
        functools.partial(_rms_proj_kernel, n_out=len(weights)),
        out_shape=tuple(jax.ShapeDtypeStruct((T, w.shape[1]), dt) for w, dt in zip(weights, out_dtypes)),
        grid=(T // tm,),
        in_specs=[pl.BlockSpec((tm, D), row), _resident((1, D))] + [_resident(w.shape) for w in weights],
        out_specs=tuple(pl.BlockSpec((tm, w.shape[1]), row) for w in weights),
        compiler_params=_params("parallel"),
        name="rms_proj",
    )(x, g.reshape(1, D), *weights)


def _mla_prep_kernel(za_ref, gq_ref, gkv_ref, wqn_ref, wqr_ref, wkn_ref, wv_ref, place_ref, ones_ref,
                     cos_ref, sin_ref, q_ref, k_ref, v_ref, *, q_scale):
    za = za_ref[...]
    cq = za[:, :A_Q_RANK]
    ckv = za[:, A_Q_RANK:A_Q_RANK + A_KV_RANK]
    kr1 = za[:, 640:768]
    kr2 = za[:, 768:896]
    cos = cos_ref[...]
    sin = sin_ref[...]

    def place(x, idx):
        return jnp.dot(x.astype(BF16), place_ref[idx], preferred_element_type=F32)

    hq = _rms(cq, gq_ref[...]).astype(BF16)
    qr = jnp.dot(hq, wqr_ref[...], preferred_element_type=F32)
    q1 = qr[:, :LANES]
    q2 = qr[:, LANES:]
    qn = jnp.dot(hq, wqn_ref[...], preferred_element_type=F32)
    q_ref[...] = (qn * q_scale + place((q1 * cos - q2 * sin) * q_scale, 0)
                  + place((q1 * sin + q2 * cos) * q_scale, 1)).astype(BF16)

    hkv = _rms(ckv, gkv_ref[...]).astype(BF16)
    kn = jnp.dot(hkv, wkn_ref[...], preferred_element_type=F32)
    k_ref[...] = (kn + place(kr1 * cos - kr2 * sin, 2) + place(kr1 * sin + kr2 * cos, 3)).astype(BF16)
    v_ref[...] = (jnp.dot(hkv, wv_ref[...], preferred_element_type=F32) + ones_ref[...]).astype(BF16)


def _mla_prep(za, gq, gkv, wqn, wqr, wkn, wv, place, ones_row, cos, sin, seq, tm):
    T = za.shape[0]
    width = A_HEADS * LANES
    nseq = seq // tm
    q_scale = (A_NOPE + A_ROPE) ** -0.5 * math.log2(math.e)
    row = lambda i: (i, 0)
    pos = lambda i: (i % nseq, 0)
    out = jax.ShapeDtypeStruct((T, width), BF16)
    return pl.pallas_call(
        functools.partial(_mla_prep_kernel, q_scale=q_scale),
        out_shape=(out, out, out),
        grid=(T // tm,),
        in_specs=[pl.BlockSpec((tm, za.shape[1]), row),
                  _resident((1, A_Q_RANK)), _resident((1, A_KV_RANK)),
                  _resident(wqn.shape), _resident(wqr.shape), _resident(wkn.shape), _resident(wv.shape),
                  _resident(place.shape), _resident(ones_row.shape),
                  pl.BlockSpec((tm, LANES), pos), pl.BlockSpec((tm, LANES), pos)],
        out_specs=(pl.BlockSpec((tm, width), row),) * 3,
        compiler_params=_params("parallel"),
        name="mla_prep",
    )(za, gq.reshape(1, -1), gkv.reshape(1, -1), wqn, wqr, wkn, wv, place, ones_row, cos, sin)


def _mla_attn_kernel(q_ref, k_ref, v_ref, o_ref, vt_ref, s_buf, p_buf, *, sub):
    tq = q_ref.shape[1]
    seq = k_ref.shape[1]
    n_sub = seq // sub

    @pl.when(pl.program_id(2) == 0)
    def _():
        def tr(c, carry):
            off = pl.multiple_of(c * sub, sub)
            vt = v_ref[0, pl.ds(off, sub), :].astype(F32).T
            vt_ref[:, pl.ds(off, sub)] = vt[:MLA_VROWS].astype(BF16)
            return carry
        lax.fori_loop(0, n_sub, tr, 0)

    qt = q_ref[0].astype(F32).T.astype(BF16)

    def qk(j, slot):
        off = pl.multiple_of(jnp.minimum(j, n_sub - 1) * sub, sub)
        s = jnp.dot(k_ref[0, pl.ds(off, sub), :], qt, preferred_element_type=F32)
        s_buf[slot] = s
        return jnp.max(s, axis=0, keepdims=True)

    def step(j, slot, m, acc, cmax):
        cmax_next = qk(j + 1, 1 - slot)
        m_new = jnp.maximum(m, cmax)
        alpha = jnp.exp2(m - m_new)
        mb = jnp.broadcast_to(m_new, (MLA_SLAB, tq))
        for r in range(sub // MLA_SLAB):
            rows = pl.ds(r * MLA_SLAB, MLA_SLAB)
            p_buf[slot, rows, :] = jnp.exp2(s_buf[slot, rows, :] - mb).astype(BF16)
        off = pl.multiple_of(j * sub, sub)
        acc = alpha * acc + jnp.dot(vt_ref[:, pl.ds(off, sub)], p_buf[slot], preferred_element_type=F32)
        return m_new, acc, cmax_next

    def body(i, carry):
        for u in range(MLA_STEPS):
            carry = step(MLA_STEPS * i + u, u % 2, *carry)
        return carry

    m0 = jnp.full((1, tq), MASK_VALUE, F32)
    acc0 = jnp.zeros((MLA_VROWS, tq), F32)
    _, acc, _ = lax.fori_loop(0, n_sub // MLA_STEPS, body, (m0, acc0, qk(0, 0)))
    ot = acc[:A_V] / acc[A_V:A_V + 1]
    ot = jnp.concatenate([ot, jnp.zeros((LANES - A_V, tq), F32)], axis=0)
    o_ref[0] = ot.T.astype(o_ref.dtype)


def _mla_attn(q, k, v, tq, sub):
    Bsz, S, width = q.shape
    H = width // LANES
    return pl.pallas_call(
        functools.partial(_mla_attn_kernel, sub=sub),
        out_shape=jax.ShapeDtypeStruct((Bsz, S, width), BF16),
        grid=(Bsz, H, S // tq),
        in_specs=[pl.BlockSpec((1, tq, LANES), lambda b, h, i: (b, i, h)),
                  pl.BlockSpec((1, S, LANES), lambda b, h, i: (b, 0, h)),
                  pl.BlockSpec((1, S, LANES), lambda b, h, i: (b, 0, h))],
        out_specs=pl.BlockSpec((1, tq, LANES), lambda b, h, i: (b, i, h)),
        scratch_shapes=[pltpu.VMEM((MLA_VROWS, S), BF16),
                        pltpu.VMEM((2, sub, tq), F32), pltpu.VMEM((2, sub, tq), BF16)],
        compiler_params=_params("parallel", "parallel", "arbitrary"),
        name="mla_attn",
    )(q, k, v)


def _chunk_cumsum(x, chunk, reverse):
    n = x.shape[0]
    row = lax.broadcasted_iota(jnp.int32, x.shape, 0) % chunk
    d = 1
    while d < chunk:
        if reverse:
            x = x + jnp.where(row < chunk - d, pltpu.roll(x, n - d, 0), 0.0)
        else:
            x = x + jnp.where(row >= d, pltpu.roll(x, d, 0), 0.0)
        d *= 2
    return x


def _hgrn_kernel(*refs, reverse, finalize):
    if finalize:
        (q_ref, z_ref, i_ref, lb_ref, prev_ref, og_ref, gout_ref, o_ref, st_ref, sall_ref) = refs
    else:
        (q_ref, z_ref, i_ref, lb_ref, o_ref, st_ref, sall_ref) = refs
    C = HGRN_CHUNK
    tb = q_ref.shape[0]
    nc = tb // C
    half = C // 2

    @pl.when(pl.program_id(2) == 0)
    def _():
        st_ref[...] = jnp.zeros_like(st_ref)

    q = q_ref[...]
    z = z_ref[...]
    lb = lb_ref[...]
    v = i_ref[...]
    f = lb + (1.0 - lb) * jax.nn.sigmoid(z)
    key = (1.0 - lb) * jax.nn.sigmoid(-z)
    b = _chunk_cumsum(jnp.log(jnp.maximum(f, TINY)), C, reverse)

    b3 = b.reshape(nc, C, 2 * B_DK)
    if reverse:
        ref3 = b3[:, half:half + 1, :]
        tot3 = b3[:, 0:1, :]
    else:
        ref3 = b3[:, half - 1:half, :]
        tot3 = b3[:, C - 1:C, :]
    qs = q.reshape(nc, C, 2 * B_DK) * jnp.exp(b3 - ref3)
    ks = key.reshape(nc, C, 2 * B_DK) * jnp.exp(ref3 - b3)
    q_dec = (qs * jnp.exp(ref3)).astype(BF16)
    k_end = (ks * jnp.exp(tot3 - ref3)).astype(BF16)
    dec = jnp.exp(tot3)
    qs = qs.astype(BF16)
    ks = ks.astype(BF16)
    v3 = v.astype(BF16).reshape(nc, C, 2 * B_DV)

    t_idx = lax.broadcasted_iota(jnp.int32, (nc, C, C), 1)
    s_idx = lax.broadcasted_iota(jnp.int32, (nc, C, C), 2)
    causal = (s_idx >= t_idx) if reverse else (s_idx <= t_idx)

    outs = []
    for h in range(2):
        sl = slice(h * B_DK, (h + 1) * B_DK)
        scores = jnp.einsum('ctk,csk->cts', qs[:, :, sl], ks[:, :, sl], preferred_element_type=F32)
        scores = jnp.where(causal, scores, 0.0).astype(BF16)
        upd = jnp.einsum('csv,csk->cvk', v3, k_end[:, :, sl], preferred_element_type=F32)
        order = range(nc - 1, -1, -1) if reverse else range(nc)
        state = st_ref[h]
        for c in order:
            sall_ref[h, c] = state.astype(BF16)
            state = state * dec[c, :, sl] + upd[c]
        st_ref[h] = state
        o_h = (jnp.einsum('cts,csv->ctv', scores, v3, preferred_element_type=F32)
               + jnp.einsum('ctk,cvk->ctv', q_dec[:, :, sl], sall_ref[h], preferred_element_type=F32))
        outs.append(o_h.reshape(tb, 2 * B_DV))
    lane = lax.broadcasted_iota(jnp.int32, (tb, 2 * B_DV), 1)
    first = lane < B_DV
    o = jnp.where(first, outs[0], outs[1])
    if not finalize:
        o_ref[...] = o
        return
    o = o + prev_ref[...]
    sq = o * o
    ms0 = jnp.sum(jnp.where(first, sq, 0.0), axis=-1, keepdims=True)
    ms1 = jnp.sum(jnp.where(first, 0.0, sq), axis=-1, keepdims=True)
    ms = jnp.where(first, ms0, ms1) * (1.0 / B_DV)
    y = o * lax.rsqrt(ms + EPS) * gout_ref[...]
    o_ref[...] = (y * jax.nn.silu(og_ref[...])).astype(o_ref.dtype)


def _hgrn_direction(zb, lb, prev, gout, seq, reverse):
    T = zb.shape[0]
    Bsz = T // seq
    tb = HGRN_BLOCK
    nblk = seq // tb
    pairs = B_HEADS // 2
    dk2, dv2 = 2 * B_DK, 2 * B_DV
    finalize = prev is not None

    def row(b, p, i):
        return b * nblk + (nblk - 1 - i if reverse else i)

    z_base = (2 if reverse else 1) * (B_HEADS * B_DK) // dk2
    i_base = 3 * (B_HEADS * B_DK) // dv2
    g_base = i_base + (B_HEADS * B_DV) // dv2
    in_specs = [pl.BlockSpec((tb, dk2), lambda b, p, i: (row(b, p, i), p)),
                pl.BlockSpec((tb, dk2), lambda b, p, i: (row(b, p, i), z_base + p)),
                pl.BlockSpec((tb, dv2), lambda b, p, i: (row(b, p, i), i_base + p)),
                pl.BlockSpec((1, dk2), lambda b, p, i: (0, p))]
    args = [zb, zb, zb, lb.reshape(1, -1)]
    if finalize:
        in_specs += [pl.BlockSpec((tb, dv2), lambda b, p, i: (row(b, p, i), p)),
                     pl.BlockSpec((tb, dv2), lambda b, p, i: (row(b, p, i), g_base + p)),
                     pl.BlockSpec((1, dv2), lambda b, p, i: (0, 0))]
        args += [prev, zb, jnp.tile(gout, 2).reshape(1, dv2)]
    return pl.pallas_call(
        functools.partial(_hgrn_kernel, reverse=reverse, finalize=finalize),
        out_shape=jax.ShapeDtypeStruct((T, B_HEADS * B_DV), BF16 if finalize else F32),
        grid=(Bsz, pairs, nblk),
        in_specs=in_specs,
        out_specs=pl.BlockSpec((tb, dv2), lambda b, p, i: (row(b, p, i), p)),
        scratch_shapes=[pltpu.VMEM((2, dv2, B_DK), F32),
                        pltpu.VMEM((2, tb // HGRN_CHUNK, dv2, B_DK), BF16)],
        compiler_params=_params("parallel", "parallel", "arbitrary"),
        name="hgrn_bwd" if reverse else "hgrn_fwd",
    )(*args)


def _t5_bucket(rel):
    nb = REL_BUCKETS // 2
    max_exact = nb // 2
    ret = (rel > 0).astype(jnp.int32) * nb
    n = jnp.abs(rel)
    large = max_exact + (jnp.log(jnp.maximum(n, 1).astype(F32) / max_exact)
                         / math.log(REL_MAX_DIST / max_exact) * (nb - max_exact)).astype(jnp.int32)
    large = jnp.minimum(large, nb - 1)
    return ret + jnp.where(n < max_exact, n, large)


def _win_kernel(sink_ref, q_ref, kp_ref, kc_ref, kn_ref, vp_ref, vc_ref, vn_ref, bias_ref, o_ref):
    G = C_HEADS // C_KV_HEADS
    blk = C_BLOCK
    n = pl.program_id(1)
    last = pl.num_programs(1) - 1
    rows, span = G * blk, 3 * blk

    r_idx = lax.broadcasted_iota(jnp.int32, (rows, span), 0) % blk
    c_idx = lax.broadcasted_iota(jnp.int32, (rows, span), 1)
    rel = c_idx - blk - r_idx
    lo = jnp.where(n == 0, blk, 0)
    hi = jnp.where(n == last, 2 * blk, span)
    valid = (jnp.abs(rel) <= C_WINDOW) & (c_idx >= lo) & (c_idx < hi)
    g_idx = lax.broadcasted_iota(jnp.int32, (rows, 1), 0) // blk

    for kvh in range(C_KV_HEADS):
        grp = slice(kvh * LANES, (kvh + 1) * LANES)
        q = jnp.concatenate([q_ref[:, (kvh * G + g) * LANES:(kvh * G + g + 1) * LANES] for g in range(G)],
                            axis=0)
        k = jnp.concatenate([kp_ref[:, grp], kc_ref[:, grp], kn_ref[:, grp]], axis=0)
        v = jnp.concatenate([vp_ref[:, grp], vc_ref[:, grp], vn_ref[:, grp]], axis=0)
        s = lax.dot_general(q, k, NT_DIMS, preferred_element_type=F32) * (C_DH ** -0.5)
        s = s + bias_ref[kvh].reshape(rows, span)
        s = jnp.where(valid, s, MASK_VALUE)
        sink = jnp.zeros((rows, 1), F32)
        for g in range(G):
            sink = jnp.where(g_idx == g, sink_ref[kvh * G + g], sink)
        m = jnp.maximum(jnp.max(s, axis=-1, keepdims=True), sink)
        p = jnp.exp(s - m)
        p = p / (jnp.sum(p, axis=-1, keepdims=True) + jnp.exp(sink - m))
        o = jnp.dot(p.astype(BF16), v, preferred_element_type=F32).astype(o_ref.dtype)
        for g in range(G):
            o_ref[:, (kvh * G + g) * LANES:(kvh * G + g + 1) * LANES] = o[g * blk:(g + 1) * blk]


def _window_gqa(zc, bias, sink, seq):
    T = zc.shape[0]
    blk = C_BLOCK
    nb = seq // blk
    qw = C_HEADS * LANES
    kw = C_KV_HEADS * LANES
    G = C_HEADS // C_KV_HEADS

    def kv_spec(col, shift):
        return pl.BlockSpec((blk, kw), lambda b, n, sref: (b * nb + jnp.clip(n + shift, 0, nb - 1), col))

    k_col, v_col = qw // kw, qw // kw + 1
    grid_spec = pltpu.PrefetchScalarGridSpec(
        num_scalar_prefetch=1,
        grid=(T // seq, nb),
        in_specs=[pl.BlockSpec((blk, qw), lambda b, n, sref: (b * nb + n, 0)),
                  kv_spec(k_col, -1), kv_spec(k_col, 0), kv_spec(k_col, 1),
                  kv_spec(v_col, -1), kv_spec(v_col, 0), kv_spec(v_col, 1),
                  pl.BlockSpec((C_KV_HEADS, G, blk, 3 * blk), lambda b, n, sref: (0, 0, 0, 0))],
        out_specs=pl.BlockSpec((blk, qw), lambda b, n, sref: (b * nb + n, 0)))
    return pl.pallas_call(
        _win_kernel,
        out_shape=jax.ShapeDtypeStruct((T, qw), BF16),
        grid_spec=grid_spec,
        compiler_params=_params("parallel", "arbitrary"),
        name="window_gqa",
    )(sink, zc, zc, zc, zc, zc, zc, zc, bias)


def _merge_kernel(x_ref, ya_ref, yb_ref, yc_ref, ga_ref, gb_ref, gc_ref,
                  wa_ref, wb_ref, wc_ref, wo_ref, o_ref):
    def branch(y_ref, gate_ref, w_ref):
        return jax.nn.sigmoid(gate_ref[...]) * jnp.dot(y_ref[...], w_ref[...], preferred_element_type=F32)

    merged = (branch(ya_ref, ga_ref, wa_ref) + branch(yb_ref, gb_ref, wb_ref)
              + branch(yc_ref, gc_ref, wc_ref))
    o_ref[...] = x_ref[...] + jnp.dot(merged.astype(BF16), wo_ref[...], preferred_element_type=F32)


def _merge(x, ya, yb, yc, gates, wa, wb, wc, wo, tm):
    T, D = x.shape
    row = lambda i: (i, 0)
    return pl.pallas_call(
        _merge_kernel,
        out_shape=jax.ShapeDtypeStruct((T, D), F32),
        grid=(T // tm,),
        in_specs=[pl.BlockSpec((tm, D), row),
                  pl.BlockSpec((tm, ya.shape[1]), row),
                  pl.BlockSpec((tm, yb.shape[1]), row),
                  pl.BlockSpec((tm, yc.shape[1]), row),
                  pl.BlockSpec((tm, D), lambda i: (i, 0)),
                  pl.BlockSpec((tm, D), lambda i: (i, 1)),
                  pl.BlockSpec((tm, D), lambda i: (i, 2)),
                  _resident(wa.shape), _resident(wb.shape), _resident(wc.shape), _resident(wo.shape)],
        out_specs=pl.BlockSpec((tm, D), row),
        compiler_params=_params("parallel"),
        name="merge",
    )(x, ya, yb, yc, gates, gates, gates, wa, wb, wc, wo)


def _cross_kernel(x_ref, g_ref, kv_ref, wq_ref, wo_ref, o_ref):
    x = x_ref[...]
    h = _rms(x, g_ref[...]).astype(BF16)
    q = jnp.dot(h, wq_ref[...], preferred_element_type=F32).astype(BF16)
    kv = kv_ref[0]
    width = X_HEADS * X_DH
    outs = []
    for hd in range(X_HEADS):
        sl = slice(hd * X_DH, (hd + 1) * X_DH)
        k = kv[:, sl]
        v = kv[:, width + hd * X_DH: width + (hd + 1) * X_DH]
        s = lax.dot_general(q[:, sl], k, NT_DIMS, preferred_element_type=F32) * (X_DH ** -0.5)
        p = jnp.exp(s - jnp.max(s, axis=-1, keepdims=True))
        p = p / jnp.sum(p, axis=-1, keepdims=True)
        outs.append(jnp.dot(p.astype(BF16), v, preferred_element_type=F32).astype(BF16))
    o = jnp.concatenate(outs, axis=-1)
    o_ref[...] = x + jnp.dot(o, wo_ref[...], preferred_element_type=F32)


def _cross(x, g, kv, wq, wo, seq, tm):
    T, D = x.shape
    nseq = seq // tm
    row = lambda i: (i, 0)
    return pl.pallas_call(
        _cross_kernel,
        out_shape=jax.ShapeDtypeStruct((T, D), F32),
        grid=(T // tm,),
        in_specs=[pl.BlockSpec((tm, D), row),
                  _resident((1, D)),
                  pl.BlockSpec((1,) + kv.shape[1:], lambda i: (i // nseq, 0, 0)),
                  _resident(wq.shape), _resident(wo.shape)],
        out_specs=pl.BlockSpec((tm, D), row),
        compiler_params=_params("parallel"),
        name="cross_attn",
    )(x, g.reshape(1, D), kv, wq, wo)


def _swiglu_kernel(x_ref, g_ref, w1_ref, w3_ref, w2_ref, gf_ref, o_ref, *, final_norm):
    x = x_ref[...]
    h = _rms(x, g_ref[...]).astype(BF16)
    a = jnp.dot(h, w1_ref[...], preferred_element_type=F32)
    b = jnp.dot(h, w3_ref[...], preferred_element_type=F32)
    u = (jax.nn.silu(a) * b).astype(BF16)
    y = x + jnp.dot(u, w2_ref[...], preferred_element_type=F32)
    if final_norm:
        y = _rms(y, gf_ref[...])
    o_ref[...] = y


def _swiglu(x, g, w1, w3, w2, g_final, final_norm, tm):
    T, D = x.shape
    row = lambda i: (i, 0)
    return pl.pallas_call(
        functools.partial(_swiglu_kernel, final_norm=final_norm),
        out_shape=jax.ShapeDtypeStruct((T, D), F32),
        grid=(T // tm,),
        in_specs=[pl.BlockSpec((tm, D), row), _resident((1, D)),
                  _resident(w1.shape), _resident(w3.shape), _resident(w2.shape), _resident((1, D))],
        out_specs=pl.BlockSpec((tm, D), row),
        compiler_params=_params("parallel"),
        name="swiglu",
    )(x, g.reshape(1, D), w1, w3, w2, g_final.reshape(1, D))


def _pad_cols(w, width):
    return jnp.pad(w, ((0, 0), (0, width - w.shape[1])))


def _group_cols(w, heads, width):
    K = w.shape[0]
    w = w.reshape(K, heads, -1)
    return jnp.pad(w, ((0, 0), (0, 0), (0, width - w.shape[2]))).reshape(K, heads * width)


def _group_rows(w, heads, width):
    N = w.shape[1]
    w = w.reshape(heads, -1, N)
    return jnp.pad(w, ((0, 0), (0, width - w.shape[1]), (0, 0))).reshape(heads * width, N)


def _split_in_proj(w):
    o = 0
    parts = []
    for n in (A_Q_RANK, A_KV_RANK, A_ROPE, 3 * B_HEADS * B_DK + 2 * B_HEADS * B_DV,
              C_HEADS * C_DH, C_KV_HEADS * C_DH, C_KV_HEADS * C_DH, 3 * w.shape[0]):
        parts.append(w[:, o:o + n])
        o += n
    cq, ckv, kr, wb, wcq, wck, wcv, wg = parts
    half = A_ROPE // 2
    wa = jnp.concatenate([cq, ckv, _pad_cols(kr[:, :half], LANES), _pad_cols(kr[:, half:], LANES)], axis=1)
    wc = jnp.concatenate([_group_cols(wcq, C_HEADS, LANES), _group_cols(wck, C_KV_HEADS, LANES),
                          _group_cols(wcv, C_KV_HEADS, LANES)], axis=1)
    return [t.astype(BF16) for t in (wa, wb, wc, wg)]


def _mla_weights(wuq, wukv):
    half = A_ROPE // 2
    wq = wuq.reshape(A_Q_RANK, A_HEADS, A_NOPE + A_ROPE)
    wqn = _group_cols(wq[:, :, :A_NOPE].reshape(A_Q_RANK, -1), A_HEADS, LANES)
    wqr = jnp.concatenate([wq[:, :, A_NOPE:A_NOPE + half].reshape(A_Q_RANK, -1),
                           wq[:, :, A_NOPE + half:].reshape(A_Q_RANK, -1)], axis=1)
    wkv = wukv.reshape(A_KV_RANK, A_HEADS, A_NOPE + A_V)
    wkn = _group_cols(wkv[:, :, :A_NOPE].reshape(A_KV_RANK, -1), A_HEADS, LANES)
    wv = _group_cols(wkv[:, :, A_NOPE:].reshape(A_KV_RANK, -1), A_HEADS, LANES)
    return [t.astype(BF16) for t in (wqn, wqr, wkn, wv)]


def _mla_placement():
    half = A_ROPE // 2
    src = jnp.arange(LANES)[:, None]
    dst = jnp.arange(A_HEADS * LANES)[None, :]
    head, lane = dst // LANES, dst % LANES
    mats = []
    for per_head, base in ((True, A_NOPE), (True, A_NOPE + half), (False, A_NOPE), (False, A_NOPE + half)):
        j = lane - base
        want = head * half + j if per_head else j
        mats.append(((j >= 0) & (j < half) & (src == want)).astype(BF16))
    ones_row = (lane == A_V).astype(F32)
    return jnp.stack(mats), ones_row


def kernel(x, mem, w_in, g_mix, a_gq, a_gkv, a_wuq, a_wukv, b_lb, b_gout, c_sink, rel_bias,
           w_br_a, w_br_b, w_br_c, w_out, g_x, g_mem, x_wq, x_wkv, x_wo, g_ffn,
           f_w1, f_w3, f_w2, g_final):
    Bsz, S, D = x.shape
    depth = w_in.shape[0]
    T = Bsz * S
    M = mem.shape[1]
    tm = ROW_TILE
    half = A_ROPE // 2

    inv = ROPE_THETA ** (-jnp.arange(half, dtype=F32) / half)
    ang = jnp.arange(S, dtype=jnp.int32).astype(F32)[:, None] * inv[None, :]
    cos = jnp.tile(jnp.cos(ang), (1, LANES // half))
    sin = jnp.tile(jnp.sin(ang), (1, LANES // half))
    place, ones_row = _mla_placement()

    sm = jax.nn.softmax(b_lb.astype(F32), axis=1)
    lower = jnp.cumsum(sm, axis=1) - sm[:, :1]

    span = 3 * C_BLOCK
    rel = jnp.arange(span)[None, :] - C_BLOCK - jnp.arange(C_BLOCK)[:, None]
    G = C_HEADS // C_KV_HEADS
    bias = jnp.transpose(rel_bias.astype(F32)[_t5_bucket(rel)], (2, 0, 1)).reshape(C_KV_HEADS, G, C_BLOCK, span)

    xt = x.reshape(T, D)
    mem2 = mem.reshape(Bsz * M, D)
    for l in range(depth):
        za, zb, zc, zg = _rms_proj(xt, g_mix[l], _split_in_proj(w_in[l]), (F32, F32, BF16, F32), IN_PROJ_TILE)

        qa, ka, va = _mla_prep(za, a_gq[l], a_gkv[l], *_mla_weights(a_wuq[l], a_wukv[l]),
                               place, ones_row, cos, sin, S, tm)
        shp = (Bsz, S, A_HEADS * LANES)
        ya = _mla_attn(qa.reshape(shp), ka.reshape(shp), va.reshape(shp), MLA_TQ, MLA_SUB).reshape(T, -1)

        of = _hgrn_direction(zb, lower[0, l], None, None, S, reverse=False)
        yb = _hgrn_direction(zb, lower[1, l], of, b_gout[l], S, reverse=True)

        yc = _window_gqa(zc, bias, c_sink[l].astype(F32), S)

        xt = _merge(xt, ya, yb, yc, zg,
                    _group_rows(w_br_a[l], A_HEADS, LANES).astype(BF16), w_br_b[l].astype(BF16),
                    _group_rows(w_br_c[l], C_HEADS, LANES).astype(BF16), w_out[l].astype(BF16), tm)

        (kvm,) = _rms_proj(mem2, g_mem[l], [x_wkv[l].astype(BF16)], (BF16,), Bsz * M)
        xt = _cross(xt, g_x[l], kvm.reshape(Bsz, M, -1), x_wq[l].astype(BF16), x_wo[l].astype(BF16), S, tm)

        xt = _swiglu(xt, g_ffn[l], f_w1[l].astype(BF16), f_w3[l].astype(BF16), f_w2[l].astype(BF16),
                     g_final, l == depth - 1, tm)
    return xt.reshape(Bsz, S, D)
```

```python
import functools
import math

import jax
import jax.numpy as jnp
from jax import lax
from jax.experimental import pallas as pl
from jax.experimental.pallas import tpu as pltpu

F32 = jnp.float32
BF16 = jnp.bfloat16

EPS = 1e-6
MASK_VALUE = -1e30
TINY = 1e-30
ROPE_THETA = 10000.0

A_HEADS, A_NOPE, A_ROPE, A_V = 8, 64, 32, 64
A_Q_RANK, A_KV_RANK = 384, 256
B_HEADS, B_DK, B_DV = 8, 128, 64
C_HEADS, C_KV_HEADS, C_DH, C_WINDOW, C_BLOCK = 8, 2, 64, 128, 128
REL_BUCKETS, REL_MAX_DIST = 32, 128
X_HEADS, X_DH = 4, 256

LANES = 128
VMEM_LIMIT = 56 * 1024 * 1024

ROW_TILE = 512
IN_PROJ_TILE = 256
MLA_TQ = 512
MLA_SUB = 512
MLA_STEPS = 2
MLA_SLAB = 16
MLA_VROWS = 80
HGRN_CHUNK = 32
HGRN_BLOCK = 512

NT_DIMS = (((1,), (1,)), ((), ()))


def _params(*sem):
    return pltpu.CompilerParams(dimension_semantics=sem, vmem_limit_bytes=VMEM_LIMIT)


def _rms(x, g):
    return x * lax.rsqrt(jnp.mean(x * x, axis=-1, keepdims=True) + EPS) * g


def _resident(shape):
    return pl.BlockSpec(shape, lambda *_: (0,) * len(shape), pipeline_mode=pl.Buffered(1))


def _rms_proj_kernel(*refs, n_out):
    x_ref, g_ref = refs[:2]
    w_refs = refs[2:2 + n_out]
    o_refs = refs[2 + n_out:]
    h = _rms(x_ref[...], g_ref[...]).astype(BF16)
    for w_ref, o_ref in zip(w_refs, o_refs):
        o_ref[...] = jnp.dot(h, w_ref[...], preferred_element_type=F32).astype(o_ref.dtype)


def _rms_proj(x, g, weights, out_dtypes, tm):
    T, D = x.shape
    row = lambda i: (i, 0)
    return pl.pallas_call(
        functools.partial(_rms_proj_kernel, n_out=len(weights)),
        out_shape=tuple(jax.ShapeDtypeStruct((T, w.shape[1]), dt) for w, dt in zip(weights, out_dtypes)),
        grid=(T // tm,),
        in_specs=[pl.BlockSpec((tm, D), row), _resident((1, D))] + [_resident(w.shape) for w in weights],
        out_specs=tuple(pl.BlockSpec((tm, w.shape[1]), row) for w in weights),
        compiler_params=_params("parallel"),
        name="rms_proj",
    )(x, g.reshape(1, D), *weights)


def _mla_prep_kernel(za_ref, gq_ref, gkv_ref, wqn_ref, wqr_ref, wkn_ref, wv_ref, place_ref, ones_ref,
                     cos_ref, sin_ref, q_ref, k_ref, v_ref, *, q_scale):
    za = za_ref[...]
    cq = za[:, :A_Q_RANK]
    ckv = za[:, A_Q_RANK:A_Q_RANK + A_KV_RANK]
    kr1 = za[:, 640:768]
    kr2 = za[:, 768:896]
    cos = cos_ref[...]
    sin = sin_ref[...]

    def place(x, idx):
        return jnp.dot(x.astype(BF16), place_ref[idx], preferred_element_type=F32)

    hq = _rms(cq, gq_ref[...]).astype(BF16)
    qr = jnp.dot(hq, wqr_ref[...], preferred_element_type=F32)
    q1 = qr[:, :LANES]
    q2 = qr[:, LANES:]
    qn = jnp.dot(hq, wqn_ref[...], preferred_element_type=F32)
    q_ref[...] = (qn * q_scale + place((q1 * cos - q2 * sin) * q_scale, 0)
                  + place((q1 * sin + q2 * cos) * q_scale, 1)).astype(BF16)

    hkv = _rms(ckv, gkv_ref[...]).astype(BF16)
    kn = jnp.dot(hkv, wkn_ref[...], preferred_element_type=F32)
    k_ref[...] = (kn + place(kr1 * cos - kr2 * sin, 2) + place(kr1 * sin + kr2 * cos, 3)).astype(BF16)
    v_ref[...] = (jnp.dot(hkv, wv_ref[...], preferred_element_type=F32) + ones_ref[...]).astype(BF16)


def _mla_prep(za, gq, gkv, wqn, wqr, wkn, wv, place, ones_row, cos, sin, seq, tm):
    T = za.shape[0]
    width = A_HEADS * LANES
    nseq = seq // tm
    q_scale = (A_NOPE + A_ROPE) ** -0.5 * math.log2(math.e)
    row = lambda i: (i, 0)
    pos = lambda i: (i % nseq, 0)
    out = jax.ShapeDtypeStruct((T, width), BF16)
    return pl.pallas_call(
        functools.partial(_mla_prep_kernel, q_scale=q_scale),
        out_shape=(out, out, out),
        grid=(T // tm,),
        in_specs=[pl.BlockSpec((tm, za.shape[1]), row),
                  _resident((1, A_Q_RANK)), _resident((1, A_KV_RANK)),
                  _resident(wqn.shape), _resident(wqr.shape), _resident(wkn.shape), _resident(wv.shape),
                  _resident(place.shape), _resident(ones_row.shape),
                  pl.BlockSpec((tm, LANES), pos), pl.BlockSpec((tm, LANES), pos)],
        out_specs=(pl.BlockSpec((tm, width), row),) * 3,
        compiler_params=_params("parallel"),
        name="mla_prep",
    )(za, gq.reshape(1, -1), gkv.reshape(1, -1), wqn, wqr, wkn, wv, place, ones_row, cos, sin)


def _mla_attn_kernel(q_ref, k_ref, v_ref, o_ref, vt_ref, s_buf, p_buf, *, sub, steps):
    tq = q_ref.shape[1]
    seq = k_ref.shape[1]
    n_sub = seq // sub

    @pl.when(pl.program_id(2) == 0)
    def _():
        def tr(c, carry):
            off = pl.multiple_of(c * sub, sub)
            vt = v_ref[0, pl.ds(off, sub), :].astype(F32).T
            vt_ref[:, pl.ds(off, sub)] = vt[:MLA_VROWS].astype(BF16)
            return carry
        lax.fori_loop(0, n_sub, tr, 0)

    qt = q_ref[0].astype(F32).T.astype(BF16)

    def qk(j, slot):
        off = pl.multiple_of(jnp.minimum(j, n_sub - 1) * sub, sub)
        s = jnp.dot(k_ref[0, pl.ds(off, sub), :], qt, preferred_element_type=F32)
        s_buf[slot] = s
        return jnp.max(s, axis=0, keepdims=True)

    def step(j, slot, m, acc, cmax):
        cmax_next = qk(j + 1, 1 - slot)
        m_new = jnp.maximum(m, cmax)
        alpha = jnp.exp2(m - m_new)
        mb = jnp.broadcast_to(m_new, (MLA_SLAB, tq))
        for r in range(sub // MLA_SLAB):
            rows = pl.ds(r * MLA_SLAB, MLA_SLAB)
            p_buf[slot, rows, :] = jnp.exp2(s_buf[slot, rows, :] - mb).astype(BF16)
        off = pl.multiple_of(j * sub, sub)
        acc = alpha * acc + jnp.dot(vt_ref[:, pl.ds(off, sub)], p_buf[slot], preferred_element_type=F32)
        return m_new, acc, cmax_next

    def body(i, carry):
        for u in range(steps):
            carry = step(steps * i + u, u % 2, *carry)
        return carry

    m0 = jnp.full((1, tq), MASK_VALUE, F32)
    acc0 = jnp.zeros((MLA_VROWS, tq), F32)
    _, acc, _ = lax.fori_loop(0, n_sub // steps, body, (m0, acc0, qk(0, 0)))
    ot = acc[:A_V] / acc[A_V:A_V + 1]
    ot = jnp.concatenate([ot, jnp.zeros((LANES - A_V, tq), F32)], axis=0)
    o_ref[0] = ot.T.astype(o_ref.dtype)


def _mla_attn_rows_kernel(q_ref, k_ref, v_ref, o_ref, s_buf, p_buf, acc_ref, *, sub, steps):
    q = q_ref[0]
    tq = q.shape[0]
    n_sub = k_ref.shape[1] // sub

    def qk(j, slot):
        off = pl.multiple_of(jnp.minimum(j, n_sub - 1) * sub, sub)
        s = lax.dot_general(q, k_ref[0, pl.ds(off, sub), :], NT_DIMS, preferred_element_type=F32)
        s_buf[slot] = s
        return jnp.max(s, axis=1, keepdims=True)

    def step(j, slot, m, cmax):
        cmax_next = qk(j + 1, 1 - slot)
        m_new = jnp.maximum(m, cmax)
        alpha = jnp.exp2(m - m_new)
        for r in range(tq // MLA_SLAB):
            rows = pl.ds(r * MLA_SLAB, MLA_SLAB)
            mb = m_new[r * MLA_SLAB:(r + 1) * MLA_SLAB]
            p_buf[slot, rows, :] = jnp.exp2(s_buf[slot, rows, :] - mb).astype(BF16)
        off = pl.multiple_of(j * sub, sub)
        pv = jnp.dot(p_buf[slot], v_ref[0, pl.ds(off, sub), :], preferred_element_type=F32)
        acc_ref[...] = alpha * acc_ref[...] + pv
        return m_new, cmax_next

    def body(i, carry):
        for u in range(steps):
            carry = step(steps * i + u, u % 2, *carry)
        return carry

    acc_ref[...] = jnp.zeros_like(acc_ref)
    m0 = jnp.full((tq, 1), MASK_VALUE, F32)
    lax.fori_loop(0, n_sub // steps, body, (m0, qk(0, 0)))
    acc = acc_ref[...]
    o_ref[0] = (acc / acc[:, A_V:A_V + 1]).astype(o_ref.dtype)


def _mla_attn(q, k, v, h0, nh, tq, sub, steps, rows_layout):
    Bsz, S, width = q.shape
    if rows_layout:
        body = _mla_attn_rows_kernel
        scratch = [pltpu.VMEM((2, tq, sub), F32), pltpu.VMEM((2, tq, sub), BF16), pltpu.VMEM((tq, LANES), F32)]
    else:
        body = _mla_attn_kernel
        scratch = [pltpu.VMEM((MLA_VROWS, S), BF16),
                   pltpu.VMEM((2, sub, tq), F32), pltpu.VMEM((2, sub, tq), BF16)]
    return pl.pallas_call(
        functools.partial(body, sub=sub, steps=steps),
        out_shape=jax.ShapeDtypeStruct((Bsz, S, nh * LANES), BF16),
        grid=(Bsz, nh, S // tq),
        in_specs=[pl.BlockSpec((1, tq, LANES), lambda b, h, i: (b, i, h0 + h)),
                  pl.BlockSpec((1, S, LANES), lambda b, h, i: (b, 0, h0 + h)),
                  pl.BlockSpec((1, S, LANES), lambda b, h, i: (b, 0, h0 + h))],
        out_specs=pl.BlockSpec((1, tq, LANES), lambda b, h, i: (b, i, h)),
        scratch_shapes=scratch,
        compiler_params=_params("parallel", "parallel", "arbitrary"),
        name="mla_attn",
    )(q, k, v)


def _chunk_cumsum(x, chunk, reverse):
    n = x.shape[0]
    row = lax.broadcasted_iota(jnp.int32, x.shape, 0) % chunk
    d = 1
    while d < chunk:
        if reverse:
            x = x + jnp.where(row < chunk - d, pltpu.roll(x, n - d, 0), 0.0)
        else:
            x = x + jnp.where(row >= d, pltpu.roll(x, d, 0), 0.0)
        d *= 2
    return x


def _hgrn_kernel(*refs, reverse, finalize):
    if finalize:
        (q_ref, z_ref, i_ref, lb_ref, prev_ref, og_ref, gout_ref, o_ref, st_ref, sall_ref) = refs
    else:
        (q_ref, z_ref, i_ref, lb_ref, o_ref, st_ref, sall_ref) = refs
    C = HGRN_CHUNK
    tb = q_ref.shape[0]
    nc = tb // C
    half = C // 2

    @pl.when(pl.program_id(2) == 0)
    def _():
        st_ref[...] = jnp.zeros_like(st_ref)

    q = q_ref[...]
    z = z_ref[...]
    lb = lb_ref[...]
    v = i_ref[...]
    f = lb + (1.0 - lb) * jax.nn.sigmoid(z)
    key = (1.0 - lb) * jax.nn.sigmoid(-z)
    b = _chunk_cumsum(jnp.log(jnp.maximum(f, TINY)), C, reverse)

    b3 = b.reshape(nc, C, 2 * B_DK)
    if reverse:
        ref3 = b3[:, half:half + 1, :]
        tot3 = b3[:, 0:1, :]
    else:
        ref3 = b3[:, half - 1:half, :]
        tot3 = b3[:, C - 1:C, :]
    qs = q.reshape(nc, C, 2 * B_DK) * jnp.exp(b3 - ref3)
    ks = key.reshape(nc, C, 2 * B_DK) * jnp.exp(ref3 - b3)
    q_dec = (qs * jnp.exp(ref3)).astype(BF16)
    k_end = (ks * jnp.exp(tot3 - ref3)).astype(BF16)
    dec = jnp.exp(tot3)
    qs = qs.astype(BF16)
    ks = ks.astype(BF16)
    v3 = v.astype(BF16).reshape(nc, C, 2 * B_DV)

    t_idx = lax.broadcasted_iota(jnp.int32, (nc, C, C), 1)
    s_idx = lax.broadcasted_iota(jnp.int32, (nc, C, C), 2)
    causal = (s_idx >= t_idx) if reverse else (s_idx <= t_idx)

    outs = []
    for h in range(2):
        sl = slice(h * B_DK, (h + 1) * B_DK)
        scores = jnp.einsum('ctk,csk->cts', qs[:, :, sl], ks[:, :, sl], preferred_element_type=F32)
        scores = jnp.where(causal, scores, 0.0).astype(BF16)
        upd = jnp.einsum('csv,csk->cvk', v3, k_end[:, :, sl], preferred_element_type=F32)
        order = range(nc - 1, -1, -1) if reverse else range(nc)
        state = st_ref[h]
        for c in order:
            sall_ref[h, c] = state.astype(BF16)
            state = state * dec[c, :, sl] + upd[c]
        st_ref[h] = state
        o_h = (jnp.einsum('cts,csv->ctv', scores, v3, preferred_element_type=F32)
               + jnp.einsum('ctk,cvk->ctv', q_dec[:, :, sl], sall_ref[h], preferred_element_type=F32))
        outs.append(o_h.reshape(tb, 2 * B_DV))
    lane = lax.broadcasted_iota(jnp.int32, (tb, 2 * B_DV), 1)
    first = lane < B_DV
    o = jnp.where(first, outs[0], outs[1])
    if not finalize:
        o_ref[...] = o
        return
    o = o + prev_ref[...]
    sq = o * o
    ms0 = jnp.sum(jnp.where(first, sq, 0.0), axis=-1, keepdims=True)
    ms1 = jnp.sum(jnp.where(first, 0.0, sq), axis=-1, keepdims=True)
    ms = jnp.where(first, ms0, ms1) * (1.0 / B_DV)
    y = o * lax.rsqrt(ms + EPS) * gout_ref[...]
    o_ref[...] = (y * jax.nn.silu(og_ref[...])).astype(o_ref.dtype)


def _hgrn_direction(zb, lb, prev, gout, seq, reverse):
    T = zb.shape[0]
    Bsz = T // seq
    tb = HGRN_BLOCK
    nblk = seq // tb
    pairs = B_HEADS // 2
    dk2, dv2 = 2 * B_DK, 2 * B_DV
    finalize = prev is not None

    def row(b, p, i):
        return b * nblk + (nblk - 1 - i if reverse else i)

    z_base = (2 if reverse else 1) * (B_HEADS * B_DK) // dk2
    i_base = 3 * (B_HEADS * B_DK) // dv2
    g_base = i_base + (B_HEADS * B_DV) // dv2
    in_specs = [pl.BlockSpec((tb, dk2), lambda b, p, i: (row(b, p, i), p)),
                pl.BlockSpec((tb, dk2), lambda b, p, i: (row(b, p, i), z_base + p)),
                pl.BlockSpec((tb, dv2), lambda b, p, i: (row(b, p, i), i_base + p)),
                pl.BlockSpec((1, dk2), lambda b, p, i: (0, p))]
    args = [zb, zb, zb, lb.reshape(1, -1)]
    if finalize:
        in_specs += [pl.BlockSpec((tb, dv2), lambda b, p, i: (row(b, p, i), p)),
                     pl.BlockSpec((tb, dv2), lambda b, p, i: (row(b, p, i), g_base + p)),
                     pl.BlockSpec((1, dv2), lambda b, p, i: (0, 0))]
        args += [prev, zb, jnp.tile(gout, 2).reshape(1, dv2)]
    return pl.pallas_call(
        functools.partial(_hgrn_kernel, reverse=reverse, finalize=finalize),
        out_shape=jax.ShapeDtypeStruct((T, B_HEADS * B_DV), BF16 if finalize else F32),
        grid=(Bsz, pairs, nblk),
        in_specs=in_specs,
        out_specs=pl.BlockSpec((tb, dv2), lambda b, p, i: (row(b, p, i), p)),
        scratch_shapes=[pltpu.VMEM((2, dv2, B_DK), F32),
                        pltpu.VMEM((2, tb // HGRN_CHUNK, dv2, B_DK), BF16)],
        compiler_params=_params("parallel", "parallel", "arbitrary"),
        name="hgrn_bwd" if reverse else "hgrn_fwd",
    )(*args)


def _t5_bucket(rel):
    nb = REL_BUCKETS // 2
    max_exact = nb // 2
    ret = (rel > 0).astype(jnp.int32) * nb
    n = jnp.abs(rel)
    large = max_exact + (jnp.log(jnp.maximum(n, 1).astype(F32) / max_exact)
                         / math.log(REL_MAX_DIST / max_exact) * (nb - max_exact)).astype(jnp.int32)
    large = jnp.minimum(large, nb - 1)
    return ret + jnp.where(n < max_exact, n, large)


def _win_kernel(sink_ref, q_ref, kp_ref, kc_ref, kn_ref, vp_ref, vc_ref, vn_ref, bias_ref, o_ref):
    G = C_HEADS // C_KV_HEADS
    blk = C_BLOCK
    n = pl.program_id(1)
    last = pl.num_programs(1) - 1
    rows, span = G * blk, 3 * blk

    r_idx = lax.broadcasted_iota(jnp.int32, (rows, span), 0) % blk
    c_idx = lax.broadcasted_iota(jnp.int32, (rows, span), 1)
    rel = c_idx - blk - r_idx
    lo = jnp.where(n == 0, blk, 0)
    hi = jnp.where(n == last, 2 * blk, span)
    valid = (jnp.abs(rel) <= C_WINDOW) & (c_idx >= lo) & (c_idx < hi)
    g_idx = lax.broadcasted_iota(jnp.int32, (rows, 1), 0) // blk

    for kvh in range(C_KV_HEADS):
        grp = slice(kvh * LANES, (kvh + 1) * LANES)
        q = jnp.concatenate([q_ref[:, (kvh * G + g) * LANES:(kvh * G + g + 1) * LANES] for g in range(G)],
                            axis=0)
        k = jnp.concatenate([kp_ref[:, grp], kc_ref[:, grp], kn_ref[:, grp]], axis=0)
        v = jnp.concatenate([vp_ref[:, grp], vc_ref[:, grp], vn_ref[:, grp]], axis=0)
        s = lax.dot_general(q, k, NT_DIMS, preferred_element_type=F32) * (C_DH ** -0.5)
        s = s + bias_ref[kvh].reshape(rows, span)
        s = jnp.where(valid, s, MASK_VALUE)
        sink = jnp.zeros((rows, 1), F32)
        for g in range(G):
            sink = jnp.where(g_idx == g, sink_ref[kvh * G + g], sink)
        m = jnp.maximum(jnp.max(s, axis=-1, keepdims=True), sink)
        p = jnp.exp(s - m)
        p = p / (jnp.sum(p, axis=-1, keepdims=True) + jnp.exp(sink - m))
        o = jnp.dot(p.astype(BF16), v, preferred_element_type=F32).astype(o_ref.dtype)
        for g in range(G):
            o_ref[:, (kvh * G + g) * LANES:(kvh * G + g + 1) * LANES] = o[g * blk:(g + 1) * blk]


def _window_gqa(zc, bias, sink, seq):
    T = zc.shape[0]
    blk = C_BLOCK
    nb = seq // blk
    qw = C_HEADS * LANES
    kw = C_KV_HEADS * LANES
    G = C_HEADS // C_KV_HEADS

    def kv_spec(col, shift):
        return pl.BlockSpec((blk, kw), lambda b, n, sref: (b * nb + jnp.clip(n + shift, 0, nb - 1), col))

    k_col, v_col = qw // kw, qw // kw + 1
    grid_spec = pltpu.PrefetchScalarGridSpec(
        num_scalar_prefetch=1,
        grid=(T // seq, nb),
        in_specs=[pl.BlockSpec((blk, qw), lambda b, n, sref: (b * nb + n, 0)),
                  kv_spec(k_col, -1), kv_spec(k_col, 0), kv_spec(k_col, 1),
                  kv_spec(v_col, -1), kv_spec(v_col, 0), kv_spec(v_col, 1),
                  pl.BlockSpec((C_KV_HEADS, G, blk, 3 * blk), lambda b, n, sref: (0, 0, 0, 0))],
        out_specs=pl.BlockSpec((blk, qw), lambda b, n, sref: (b * nb + n, 0)))
    return pl.pallas_call(
        _win_kernel,
        out_shape=jax.ShapeDtypeStruct((T, qw), BF16),
        grid_spec=grid_spec,
        compiler_params=_params("parallel", "arbitrary"),
        name="window_gqa",
    )(sink, zc, zc, zc, zc, zc, zc, zc, bias)


def _merge_kernel(x_ref, ya_ref, yb_ref, yc_ref, ga_ref, gb_ref, gc_ref,
                  wa_ref, wb_ref, wc_ref, wo_ref, o_ref):
    def branch(y_ref, gate_ref, w_ref):
        return jax.nn.sigmoid(gate_ref[...]) * jnp.dot(y_ref[...], w_ref[...], preferred_element_type=F32)

    merged = (branch(ya_ref, ga_ref, wa_ref) + branch(yb_ref, gb_ref, wb_ref)
              + branch(yc_ref, gc_ref, wc_ref))
    o_ref[...] = x_ref[...] + jnp.dot(merged.astype(BF16), wo_ref[...], preferred_element_type=F32)


def _merge(x, ya, yb, yc, gates, wa, wb, wc, wo, tm):
    T, D = x.shape
    row = lambda i: (i, 0)
    return pl.pallas_call(
        _merge_kernel,
        out_shape=jax.ShapeDtypeStruct((T, D), F32),
        grid=(T // tm,),
        in_specs=[pl.BlockSpec((tm, D), row),
                  pl.BlockSpec((tm, ya.shape[1]), row),
                  pl.BlockSpec((tm, yb.shape[1]), row),
                  pl.BlockSpec((tm, yc.shape[1]), row),
                  pl.BlockSpec((tm, D), lambda i: (i, 0)),
                  pl.BlockSpec((tm, D), lambda i: (i, 1)),
                  pl.BlockSpec((tm, D), lambda i: (i, 2)),
                  _resident(wa.shape), _resident(wb.shape), _resident(wc.shape), _resident(wo.shape)],
        out_specs=pl.BlockSpec((tm, D), row),
        compiler_params=_params("parallel"),
        name="merge",
    )(x, ya, yb, yc, gates, gates, gates, wa, wb, wc, wo)


def _cross_kernel(x_ref, g_ref, kv_ref, wq_ref, wo_ref, o_ref):
    x = x_ref[...]
    h = _rms(x, g_ref[...]).astype(BF16)
    q = jnp.dot(h, wq_ref[...], preferred_element_type=F32).astype(BF16)
    kv = kv_ref[0]
    width = X_HEADS * X_DH
    outs = []
    for hd in range(X_HEADS):
        sl = slice(hd * X_DH, (hd + 1) * X_DH)
        k = kv[:, sl]
        v = kv[:, width + hd * X_DH: width + (hd + 1) * X_DH]
        s = lax.dot_general(q[:, sl], k, NT_DIMS, preferred_element_type=F32) * (X_DH ** -0.5)
        p = jnp.exp(s - jnp.max(s, axis=-1, keepdims=True))
        p = p / jnp.sum(p, axis=-1, keepdims=True)
        outs.append(jnp.dot(p.astype(BF16), v, preferred_element_type=F32).astype(BF16))
    o = jnp.concatenate(outs, axis=-1)
    o_ref[...] = x + jnp.dot(o, wo_ref[...], preferred_element_type=F32)


def _cross(x, g, kv, wq, wo, seq, tm):
    T, D = x.shape
    nseq = seq // tm
    row = lambda i: (i, 0)
    return pl.pallas_call(
        _cross_kernel,
        out_shape=jax.ShapeDtypeStruct((T, D), F32),
        grid=(T // tm,),
        in_specs=[pl.BlockSpec((tm, D), row),
                  _resident((1, D)),
                  pl.BlockSpec((1,) + kv.shape[1:], lambda i: (i // nseq, 0, 0)),
                  _resident(wq.shape), _resident(wo.shape)],
        out_specs=pl.BlockSpec((tm, D), row),
        compiler_params=_params("parallel"),
        name="cross_attn",
    )(x, g.reshape(1, D), kv, wq, wo)


def _swiglu_kernel(x_ref, g_ref, w1_ref, w3_ref, w2_ref, gf_ref, o_ref, *, final_norm):
    x = x_ref[...]
    h = _rms(x, g_ref[...]).astype(BF16)
    a = jnp.dot(h, w1_ref[...], preferred_element_type=F32)
    b = jnp.dot(h, w3_ref[...], preferred_element_type=F32)
    u = (jax.nn.silu(a) * b).astype(BF16)
    y = x + jnp.dot(u, w2_ref[...], preferred_element_type=F32)
    if final_norm:
        y = _rms(y, gf_ref[...])
    o_ref[...] = y


def _swiglu(x, g, w1, w3, w2, g_final, final_norm, tm):
    T, D = x.shape
    row = lambda i: (i, 0)
    return pl.pallas_call(
        functools.partial(_swiglu_kernel, final_norm=final_norm),
        out_shape=jax.ShapeDtypeStruct((T, D), F32),
        grid=(T // tm,),
        in_specs=[pl.BlockSpec((tm, D), row), _resident((1, D)),
                  _resident(w1.shape), _resident(w3.shape), _resident(w2.shape), _resident((1, D))],
        out_specs=pl.BlockSpec((tm, D), row),
        compiler_params=_params("parallel"),
        name="swiglu",
    )(x, g.reshape(1, D), w1, w3, w2, g_final.reshape(1, D))


def _pad_cols(w, width):
    return jnp.pad(w, ((0, 0), (0, width - w.shape[1])))


def _group_cols(w, heads, width):
    K = w.shape[0]
    w = w.reshape(K, heads, -1)
    return jnp.pad(w, ((0, 0), (0, 0), (0, width - w.shape[2]))).reshape(K, heads * width)


def _group_rows(w, heads, width):
    N = w.shape[1]
    w = w.reshape(heads, -1, N)
    return jnp.pad(w, ((0, 0), (0, width - w.shape[1]), (0, 0))).reshape(heads * width, N)


def _split_in_proj(w):
    o = 0
    parts = []
    for n in (A_Q_RANK, A_KV_RANK, A_ROPE, 3 * B_HEADS * B_DK + 2 * B_HEADS * B_DV,
              C_HEADS * C_DH, C_KV_HEADS * C_DH, C_KV_HEADS * C_DH, 3 * w.shape[0]):
        parts.append(w[:, o:o + n])
        o += n
    cq, ckv, kr, wb, wcq, wck, wcv, wg = parts
    half = A_ROPE // 2
    wa = jnp.concatenate([cq, ckv, _pad_cols(kr[:, :half], LANES), _pad_cols(kr[:, half:], LANES)], axis=1)
    wc = jnp.concatenate([_group_cols(wcq, C_HEADS, LANES), _group_cols(wck, C_KV_HEADS, LANES),
                          _group_cols(wcv, C_KV_HEADS, LANES)], axis=1)
    return [t.astype(BF16) for t in (wa, wb, wc, wg)]


def _mla_weights(wuq, wukv):
    half = A_ROPE // 2
    wq = wuq.reshape(A_Q_RANK, A_HEADS, A_NOPE + A_ROPE)
    wqn = _group_cols(wq[:, :, :A_NOPE].reshape(A_Q_RANK, -1), A_HEADS, LANES)
    wqr = jnp.concatenate([wq[:, :, A_NOPE:A_NOPE + half].reshape(A_Q_RANK, -1),
                           wq[:, :, A_NOPE + half:].reshape(A_Q_RANK, -1)], axis=1)
    wkv = wukv.reshape(A_KV_RANK, A_HEADS, A_NOPE + A_V)
    wkn = _group_cols(wkv[:, :, :A_NOPE].reshape(A_KV_RANK, -1), A_HEADS, LANES)
    wv = _group_cols(wkv[:, :, A_NOPE:].reshape(A_KV_RANK, -1), A_HEADS, LANES)
    return [t.astype(BF16) for t in (wqn, wqr, wkn, wv)]


def _mla_placement():
    half = A_ROPE // 2
    src = jnp.arange(LANES)[:, None]
    dst = jnp.arange(A_HEADS * LANES)[None, :]
    head, lane = dst // LANES, dst % LANES
    mats = []
    for per_head, base in ((True, A_NOPE), (True, A_NOPE + half), (False, A_NOPE), (False, A_NOPE + half)):
        j = lane - base
        want = head * half + j if per_head else j
        mats.append(((j >= 0) & (j < half) & (src == want)).astype(BF16))
    ones_row = (lane == A_V).astype(F32)
    return jnp.stack(mats), ones_row


def kernel(x, mem, w_in, g_mix, a_gq, a_gkv, a_wuq, a_wukv, b_lb, b_gout, c_sink, rel_bias,
           w_br_a, w_br_b, w_br_c, w_out, g_x, g_mem, x_wq, x_wkv, x_wo, g_ffn,
           f_w1, f_w3, f_w2, g_final):
    Bsz, S, D = x.shape
    depth = w_in.shape[0]
    T = Bsz * S
    M = mem.shape[1]
    tm = ROW_TILE
    half = A_ROPE // 2

    inv = ROPE_THETA ** (-jnp.arange(half, dtype=F32) / half)
    ang = jnp.arange(S, dtype=jnp.int32).astype(F32)[:, None] * inv[None, :]
    cos = jnp.tile(jnp.cos(ang), (1, LANES // half))
    sin = jnp.tile(jnp.sin(ang), (1, LANES // half))
    place, ones_row = _mla_placement()

    sm = jax.nn.softmax(b_lb.astype(F32), axis=1)
    lower = jnp.cumsum(sm, axis=1) - sm[:, :1]

    span = 3 * C_BLOCK
    rel = jnp.arange(span)[None, :] - C_BLOCK - jnp.arange(C_BLOCK)[:, None]
    G = C_HEADS // C_KV_HEADS
    onehot = (_t5_bucket(rel)[None] == jnp.arange(REL_BUCKETS)[:, None, None]).astype(F32)
    bias = jnp.einsum('nh,nqs->hqs', rel_bias.astype(F32), onehot,
                      precision=lax.Precision.HIGHEST).reshape(C_KV_HEADS, G, C_BLOCK, span)

    xt = x.reshape(T, D)
    mem2 = mem.reshape(Bsz * M, D)
    for l in range(depth):
        za, zb, zc, zg = _rms_proj(xt, g_mix[l], _split_in_proj(w_in[l]), (F32, F32, BF16, F32), IN_PROJ_TILE)

        qa, ka, va = _mla_prep(za, a_gq[l], a_gkv[l], *_mla_weights(a_wuq[l], a_wukv[l]),
                               place, ones_row, cos, sin, S, tm)
        shp = (Bsz, S, A_HEADS * LANES)
        qkv = (qa.reshape(shp), ka.reshape(shp), va.reshape(shp))
        half_heads = A_HEADS // 2
        variants = (((512, 512, 2, False), (1024, 512, 4, False)), ((512, 512, 2, True), (512, 1024, 2, False)))[l]
        ya = jnp.concatenate([_mla_attn(*qkv, i * half_heads, half_heads, *cfg) for i, cfg in enumerate(variants)],
                             axis=-1).reshape(T, -1)

        of = _hgrn_direction(zb, lower[0, l], None, None, S, reverse=False)
        yb = _hgrn_direction(zb, lower[1, l], of, b_gout[l], S, reverse=True)

        yc = _window_gqa(zc, bias, c_sink[l].astype(F32), S)

        xt = _merge(xt, ya, yb, yc, zg,
                    _group_rows(w_br_a[l], A_HEADS, LANES).astype(BF16), w_br_b[l].astype(BF16),
                    _group_rows(w_br_c[l], C_HEADS, LANES).astype(BF16), w_out[l].astype(BF16), tm)

        (kvm,) = _rms_proj(mem2, g_mem[l], [x_wkv[l].astype(BF16)], (BF16,), Bsz * M)
        xt = _cross(xt, g_x[l], kvm.reshape(Bsz, M, -1), x_wq[l].astype(BF16), x_wo[l].astype(BF16), S, tm)

        xt = _swiglu(xt, g_ffn[l], f_w1[l].astype(BF16), f_w3[l].astype(BF16), f_w2[l].astype(BF16),
                     g_final, l == depth - 1, tm)
    return xt.reshape(Bsz, S, D)
```

```python
import functools
import math

import jax
import jax.numpy as jnp
from jax import lax
from jax.experimental import pallas as pl
from jax.experimental.pallas import tpu as pltpu

F32 = jnp.float32
BF16 = jnp.bfloat16

EPS = 1e-6
MASK_VALUE = -1e30
TINY = 1e-30
ROPE_THETA = 10000.0

A_HEADS, A_NOPE, A_ROPE, A_V = 8, 64, 32, 64
A_Q_RANK, A_KV_RANK = 384, 256
B_HEADS, B_DK, B_DV = 8, 128, 64
C_HEADS, C_KV_HEADS, C_DH, C_WINDOW, C_BLOCK = 8, 2, 64, 128, 128
REL_BUCKETS, REL_MAX_DIST = 32, 128
X_HEADS, X_DH = 4, 256

LANES = 128
VMEM_LIMIT = 56 * 1024 * 1024

ROW_TILE = 512
IN_PROJ_TILE = 256
MLA_TQ = 1024
MLA_SUB = 256
MLA_STEPS = 16
MLA_SLAB = 16
MLA_VROWS = 80
HGRN_CHUNK = 32
HGRN_BLOCK = 512

NT_DIMS = (((1,), (1,)), ((), ()))


def _params(*sem):
    return pltpu.CompilerParams(dimension_semantics=sem, vmem_limit_bytes=VMEM_LIMIT)


def _rms(x, g):
    return x * lax.rsqrt(jnp.mean(x * x, axis=-1, keepdims=True) + EPS) * g


def _resident(shape):
    return pl.BlockSpec(shape, lambda *_: (0,) * len(shape), pipeline_mode=pl.Buffered(1))


def _rms_proj_kernel(*refs, n_out):
    x_ref, g_ref = refs[:2]
    w_refs = refs[2:2 + n_out]
    o_refs = refs[2 + n_out:]
    h = _rms(x_ref[...], g_ref[...]).astype(BF16)
    for w_ref, o_ref in zip(w_refs, o_refs):
        o_ref[...] = jnp.dot(h, w_ref[...], preferred_element_type=F32).astype(o_ref.dtype)


def _rms_proj(x, g, weights, out_dtypes, tm):
    T, D = x.shape
    row = lambda i: (i, 0)
    return pl.pallas_call(
        functools.partial(_rms_proj_kernel, n_out=len(weights)),
        out_shape=tuple(jax.ShapeDtypeStruct((T, w.shape[1]), dt) for w, dt in zip(weights, out_dtypes)),
        grid=(T // tm,),
        in_specs=[pl.BlockSpec((tm, D), row), _resident((1, D))] + [_resident(w.shape) for w in weights],
        out_specs=tuple(pl.BlockSpec((tm, w.shape[1]), row) for w in weights),
        compiler_params=_params("parallel"),
        name="rms_proj",
    )(x, g.reshape(1, D), *weights)


def _mla_prep_kernel(za_ref, gq_ref, gkv_ref, wqn_ref, wqr_ref, wkn_ref, wv_ref, place_ref, ones_ref,
                     cos_ref, sin_ref, q_ref, k_ref, v_ref, *, q_scale):
    za = za_ref[...]
    cq = za[:, :A_Q_RANK]
    ckv = za[:, A_Q_RANK:A_Q_RANK + A_KV_RANK]
    kr1 = za[:, 640:768]
    kr2 = za[:, 768:896]
    cos = cos_ref[...]
    sin = sin_ref[...]

    def place(x, idx):
        return jnp.dot(x.astype(BF16), place_ref[idx], preferred_element_type=F32)

    hq = _rms(cq, gq_ref[...]).astype(BF16)
    qr = jnp.dot(hq, wqr_ref[...], preferred_element_type=F32)
    q1 = qr[:, :LANES]
    q2 = qr[:, LANES:]
    qn = jnp.dot(hq, wqn_ref[...], preferred_element_type=F32)
    q_ref[...] = (qn * q_scale + place((q1 * cos - q2 * sin) * q_scale, 0)
                  + place((q1 * sin + q2 * cos) * q_scale, 1)).astype(BF16)

    hkv = _rms(ckv, gkv_ref[...]).astype(BF16)
    kn = jnp.dot(hkv, wkn_ref[...], preferred_element_type=F32)
    k_ref[...] = (kn + place(kr1 * cos - kr2 * sin, 2) + place(kr1 * sin + kr2 * cos, 3)).astype(BF16)
    v_ref[...] = (jnp.dot(hkv, wv_ref[...], preferred_element_type=F32) + ones_ref[...]).astype(BF16)


def _mla_prep(za, gq, gkv, wqn, wqr, wkn, wv, place, ones_row, cos, sin, seq, tm):
    T = za.shape[0]
    width = A_HEADS * LANES
    nseq = seq // tm
    q_scale = (A_NOPE + A_ROPE) ** -0.5 * math.log2(math.e)
    row = lambda i: (i, 0)
    pos = lambda i: (i % nseq, 0)
    out = jax.ShapeDtypeStruct((T, width), BF16)
    return pl.pallas_call(
        functools.partial(_mla_prep_kernel, q_scale=q_scale),
        out_shape=(out, out, out),
        grid=(T // tm,),
        in_specs=[pl.BlockSpec((tm, za.shape[1]), row),
                  _resident((1, A_Q_RANK)), _resident((1, A_KV_RANK)),
                  _resident(wqn.shape), _resident(wqr.shape), _resident(wkn.shape), _resident(wv.shape),
                  _resident(place.shape), _resident(ones_row.shape),
                  pl.BlockSpec((tm, LANES), pos), pl.BlockSpec((tm, LANES), pos)],
        out_specs=(pl.BlockSpec((tm, width), row),) * 3,
        compiler_params=_params("parallel"),
        name="mla_prep",
    )(za, gq.reshape(1, -1), gkv.reshape(1, -1), wqn, wqr, wkn, wv, place, ones_row, cos, sin)


def _mla_attn_kernel(q_ref, k_ref, v_ref, o_ref, vt_ref, s_buf, p_buf, *, sub, steps):
    tq = q_ref.shape[1]
    seq = k_ref.shape[1]
    n_sub = seq // sub

    @pl.when(pl.program_id(2) == 0)
    def _():
        def tr(c, carry):
            off = pl.multiple_of(c * sub, sub)
            vt = v_ref[0, pl.ds(off, sub), :].astype(F32).T
            vt_ref[:, pl.ds(off, sub)] = vt[:MLA_VROWS].astype(BF16)
            return carry
        lax.fori_loop(0, n_sub, tr, 0)

    qt = q_ref[0].astype(F32).T.astype(BF16)

    def qk(j, slot):
        off = pl.multiple_of(jnp.minimum(j, n_sub - 1) * sub, sub)
        s = jnp.dot(k_ref[0, pl.ds(off, sub), :], qt, preferred_element_type=F32)
        s_buf[slot] = s
        return jnp.max(s, axis=0, keepdims=True)

    def step(j, slot, m, acc, cmax):
        cmax_next = qk(j + 1, 1 - slot)
        m_new = jnp.maximum(m, cmax)
        alpha = jnp.exp2(m - m_new)
        mb = jnp.broadcast_to(m_new, (MLA_SLAB, tq))
        for r in range(sub // MLA_SLAB):
            rows = pl.ds(r * MLA_SLAB, MLA_SLAB)
            p_buf[slot, rows, :] = jnp.exp2(s_buf[slot, rows, :] - mb).astype(BF16)
        off = pl.multiple_of(j * sub, sub)
        acc = alpha * acc + jnp.dot(vt_ref[:, pl.ds(off, sub)], p_buf[slot], preferred_element_type=F32)
        return m_new, acc, cmax_next

    def body(i, carry):
        for u in range(steps):
            carry = step(steps * i + u, u % 2, *carry)
        return carry

    m0 = jnp.full((1, tq), MASK_VALUE, F32)
    acc0 = jnp.zeros((MLA_VROWS, tq), F32)
    _, acc, _ = lax.fori_loop(0, n_sub // steps, body, (m0, acc0, qk(0, 0)))
    ot = acc[:A_V] / acc[A_V:A_V + 1]
    ot = jnp.concatenate([ot, jnp.zeros((LANES - A_V, tq), F32)], axis=0)
    o_ref[0] = ot.T.astype(o_ref.dtype)


def _mla_attn(q, k, v, tq, sub, steps):
    Bsz, S, width = q.shape
    assert S % tq == 0 and S % (sub * steps) == 0 and steps % 2 == 0
    return pl.pallas_call(
        functools.partial(_mla_attn_kernel, sub=sub, steps=steps),
        out_shape=jax.ShapeDtypeStruct((Bsz, S, width), BF16),
        grid=(Bsz, width // LANES, S // tq),
        in_specs=[pl.BlockSpec((1, tq, LANES), lambda b, h, i: (b, i, h)),
                  pl.BlockSpec((1, S, LANES), lambda b, h, i: (b, 0, h)),
                  pl.BlockSpec((1, S, LANES), lambda b, h, i: (b, 0, h))],
        out_specs=pl.BlockSpec((1, tq, LANES), lambda b, h, i: (b, i, h)),
        scratch_shapes=[pltpu.VMEM((MLA_VROWS, S), BF16),
                        pltpu.VMEM((2, sub, tq), F32), pltpu.VMEM((2, sub, tq), BF16)],
        compiler_params=_params("parallel", "parallel", "arbitrary"),
        name="mla_attn",
    )(q, k, v)


def _chunk_cumsum(x, chunk, reverse):
    n, w = x.shape
    sub = 8
    per = chunk // sub
    x3 = x.reshape(n // sub, sub, w)
    row = lax.broadcasted_iota(jnp.int32, x3.shape, 1)
    d = 1
    while d < sub:
        if reverse:
            x3 = x3 + jnp.where(row < sub - d, pltpu.roll(x3, sub - d, 1), 0.0)
        else:
            x3 = x3 + jnp.where(row >= d, pltpu.roll(x3, d, 1), 0.0)
        d *= 2
    xc = x3.reshape(n // chunk, chunk, w)
    parts = [xc[:, i * sub:(i + 1) * sub, :] for i in range(per)]
    edge = 0 if reverse else sub - 1
    carry = None
    for i in (range(per - 1, -1, -1) if reverse else range(per)):
        total = parts[i][:, edge:edge + 1, :]
        if carry is not None:
            parts[i] = parts[i] + carry
            carry = carry + total
        else:
            carry = total
    return jnp.concatenate(parts, axis=1).reshape(n, w)


def _hgrn_kernel(*refs, reverse, finalize):
    if finalize:
        (q_ref, z_ref, i_ref, lb_ref, prev_ref, og_ref, gout_ref, o_ref, st_ref, sall_ref) = refs
    else:
        (q_ref, z_ref, i_ref, lb_ref, o_ref, st_ref, sall_ref) = refs
    C = HGRN_CHUNK
    tb = q_ref.shape[0]
    nc = tb // C
    half = C // 2

    @pl.when(pl.program_id(2) == 0)
    def _():
        st_ref[...] = jnp.zeros_like(st_ref)

    q = q_ref[...]
    z = z_ref[...]
    lb = lb_ref[...]
    v = i_ref[...]
    sg = jax.nn.sigmoid(z)
    f = lb + (1.0 - lb) * sg
    key = (1.0 - lb) * (1.0 - sg)
    b = _chunk_cumsum(jnp.log(jnp.maximum(f, TINY)), C, reverse)

    b3 = b.reshape(nc, C, 2 * B_DK)
    if reverse:
        ref3 = b3[:, half:half + 1, :]
        tot3 = b3[:, 0:1, :]
    else:
        ref3 = b3[:, half - 1:half, :]
        tot3 = b3[:, C - 1:C, :]
    qs = q.reshape(nc, C, 2 * B_DK) * jnp.exp(b3 - ref3)
    ks = key.reshape(nc, C, 2 * B_DK) * jnp.exp(ref3 - b3)
    q_dec = (qs * jnp.exp(ref3)).astype(BF16)
    k_end = (ks * jnp.exp(tot3 - ref3)).astype(BF16)
    dec = jnp.exp(tot3)
    qs = qs.astype(BF16)
    ks = ks.astype(BF16)
    v3 = v.astype(BF16).reshape(nc, C, 2 * B_DV)

    t_idx = lax.broadcasted_iota(jnp.int32, (nc, C, C), 1)
    s_idx = lax.broadcasted_iota(jnp.int32, (nc, C, C), 2)
    causal = (s_idx >= t_idx) if reverse else (s_idx <= t_idx)

    outs = []
    for h in range(2):
        sl = slice(h * B_DK, (h + 1) * B_DK)
        scores = jnp.einsum('ctk,csk->cts', qs[:, :, sl], ks[:, :, sl], preferred_element_type=F32)
        scores = jnp.where(causal, scores, 0.0).astype(BF16)
        upd = jnp.einsum('csv,csk->cvk', v3, k_end[:, :, sl], preferred_element_type=F32)
        order = range(nc - 1, -1, -1) if reverse else range(nc)
        state = st_ref[h]
        for c in order:
            sall_ref[h, c] = state.astype(BF16)
            state = state * dec[c, :, sl] + upd[c]
        st_ref[h] = state
        o_h = (jnp.einsum('cts,csv->ctv', scores, v3, preferred_element_type=F32)
               + jnp.einsum('ctk,cvk->ctv', q_dec[:, :, sl], sall_ref[h], preferred_element_type=F32))
        outs.append(o_h.reshape(tb, 2 * B_DV))
    lane = lax.broadcasted_iota(jnp.int32, (tb, 2 * B_DV), 1)
    first = lane < B_DV
    o = jnp.where(first, outs[0], outs[1])
    if not finalize:
        o_ref[...] = o
        return
    o = o + prev_ref[...]
    sq = o * o
    ms0 = jnp.sum(jnp.where(first, sq, 0.0), axis=-1, keepdims=True)
    ms1 = jnp.sum(jnp.where(first, 0.0, sq), axis=-1, keepdims=True)
    ms = jnp.where(first, ms0, ms1) * (1.0 / B_DV)
    y = o * lax.rsqrt(ms + EPS) * gout_ref[...]
    o_ref[...] = (y * jax.nn.silu(og_ref[...])).astype(o_ref.dtype)


def _hgrn_direction(zb, lb, prev, gout, seq, reverse):
    T = zb.shape[0]
    Bsz = T // seq
    tb = HGRN_BLOCK
    nblk = seq // tb
    pairs = B_HEADS // 2
    dk2, dv2 = 2 * B_DK, 2 * B_DV
    finalize = prev is not None

    def row(b, p, i):
        return b * nblk + (nblk - 1 - i if reverse else i)

    z_base = (2 if reverse else 1) * (B_HEADS * B_DK) // dk2
    i_base = 3 * (B_HEADS * B_DK) // dv2
    g_base = i_base + (B_HEADS * B_DV) // dv2
    in_specs = [pl.BlockSpec((tb, dk2), lambda b, p, i: (row(b, p, i), p)),
                pl.BlockSpec((tb, dk2), lambda b, p, i: (row(b, p, i), z_base + p)),
                pl.BlockSpec((tb, dv2), lambda b, p, i: (row(b, p, i), i_base + p)),
                pl.BlockSpec((1, dk2), lambda b, p, i: (0, p))]
    args = [zb, zb, zb, lb.reshape(1, -1)]
    if finalize:
        in_specs += [pl.BlockSpec((tb, dv2), lambda b, p, i: (row(b, p, i), p)),
                     pl.BlockSpec((tb, dv2), lambda b, p, i: (row(b, p, i), g_base + p)),
                     pl.BlockSpec((1, dv2), lambda b, p, i: (0, 0))]
        args += [prev, zb, jnp.tile(gout, 2).reshape(1, dv2)]
    return pl.pallas_call(
        functools.partial(_hgrn_kernel, reverse=reverse, finalize=finalize),
        out_shape=jax.ShapeDtypeStruct((T, B_HEADS * B_DV), BF16 if finalize else F32),
        grid=(Bsz, pairs, nblk),
        in_specs=in_specs,
        out_specs=pl.BlockSpec((tb, dv2), lambda b, p, i: (row(b, p, i), p)),
        scratch_shapes=[pltpu.VMEM((2, dv2, B_DK), F32),
                        pltpu.VMEM((2, tb // HGRN_CHUNK, dv2, B_DK), BF16)],
        compiler_params=_params("parallel", "parallel", "arbitrary"),
        name="hgrn_bwd" if reverse else "hgrn_fwd",
    )(*args)


def _t5_bucket(rel):
    nb = REL_BUCKETS // 2
    max_exact = nb // 2
    ret = (rel > 0).astype(jnp.int32) * nb
    n = jnp.abs(rel)
    large = max_exact + (jnp.log(jnp.maximum(n, 1).astype(F32) / max_exact)
                         / math.log(REL_MAX_DIST / max_exact) * (nb - max_exact)).astype(jnp.int32)
    large = jnp.minimum(large, nb - 1)
    return ret + jnp.where(n < max_exact, n, large)


def _win_kernel(sink_ref, q_ref, kp_ref, kc_ref, kn_ref, vp_ref, vc_ref, vn_ref, bias_ref, o_ref):
    G = C_HEADS // C_KV_HEADS
    blk = C_BLOCK
    n = pl.program_id(1)
    last = pl.num_programs(1) - 1
    rows, span = G * blk, 3 * blk

    r_idx = lax.broadcasted_iota(jnp.int32, (rows, span), 0) % blk
    c_idx = lax.broadcasted_iota(jnp.int32, (rows, span), 1)
    rel = c_idx - blk - r_idx
    lo = jnp.where(n == 0, blk, 0)
    hi = jnp.where(n == last, 2 * blk, span)
    valid = (jnp.abs(rel) <= C_WINDOW) & (c_idx >= lo) & (c_idx < hi)
    g_idx = lax.broadcasted_iota(jnp.int32, (rows, 1), 0) // blk

    for kvh in range(C_KV_HEADS):
        grp = slice(kvh * LANES, (kvh + 1) * LANES)
        q = jnp.concatenate([q_ref[:, (kvh * G + g) * LANES:(kvh * G + g + 1) * LANES] for g in range(G)],
                            axis=0)
        k = jnp.concatenate([kp_ref[:, grp], kc_ref[:, grp], kn_ref[:, grp]], axis=0)
        v = jnp.concatenate([vp_ref[:, grp], vc_ref[:, grp], vn_ref[:, grp]], axis=0)
        s = lax.dot_general(q, k, NT_DIMS, preferred_element_type=F32) * (C_DH ** -0.5)
        s = s + bias_ref[kvh].reshape(rows, span)
        s = jnp.where(valid, s, MASK_VALUE)
        sink = jnp.zeros((rows, 1), F32)
        for g in range(G):
            sink = jnp.where(g_idx == g, sink_ref[kvh * G + g], sink)
        m = jnp.maximum(jnp.max(s, axis=-1, keepdims=True), sink)
        p = jnp.exp(s - m)
        p = p / (jnp.sum(p, axis=-1, keepdims=True) + jnp.exp(sink - m))
        o = jnp.dot(p.astype(BF16), v, preferred_element_type=F32).astype(o_ref.dtype)
        for g in range(G):
            o_ref[:, (kvh * G + g) * LANES:(kvh * G + g + 1) * LANES] = o[g * blk:(g + 1) * blk]


def _window_gqa(zc, bias, sink, seq):
    T = zc.shape[0]
    blk = C_BLOCK
    nb = seq // blk
    qw = C_HEADS * LANES
    kw = C_KV_HEADS * LANES
    G = C_HEADS // C_KV_HEADS

    def kv_spec(col, shift):
        return pl.BlockSpec((blk, kw), lambda b, n, sref: (b * nb + jnp.clip(n + shift, 0, nb - 1), col))

    k_col, v_col = qw // kw, qw // kw + 1
    grid_spec = pltpu.PrefetchScalarGridSpec(
        num_scalar_prefetch=1,
        grid=(T // seq, nb),
        in_specs=[pl.BlockSpec((blk, qw), lambda b, n, sref: (b * nb + n, 0)),
                  kv_spec(k_col, -1), kv_spec(k_col, 0), kv_spec(k_col, 1),
                  kv_spec(v_col, -1), kv_spec(v_col, 0), kv_spec(v_col, 1),
                  pl.BlockSpec((C_KV_HEADS, G, blk, 3 * blk), lambda b, n, sref: (0, 0, 0, 0))],
        out_specs=pl.BlockSpec((blk, qw), lambda b, n, sref: (b * nb + n, 0)))
    return pl.pallas_call(
        _win_kernel,
        out_shape=jax.ShapeDtypeStruct((T, qw), BF16),
        grid_spec=grid_spec,
        compiler_params=_params("parallel", "arbitrary"),
        name="window_gqa",
    )(sink, zc, zc, zc, zc, zc, zc, zc, bias)


def _merge_kernel(x_ref, ya_ref, yb_ref, yc_ref, ga_ref, gb_ref, gc_ref,
                  wa_ref, wb_ref, wc_ref, wo_ref, o_ref):
    def branch(y_ref, gate_ref, w_ref):
        return jax.nn.sigmoid(gate_ref[...]) * jnp.dot(y_ref[...], w_ref[...], preferred_element_type=F32)

    merged = (branch(ya_ref, ga_ref, wa_ref) + branch(yb_ref, gb_ref, wb_ref)
              + branch(yc_ref, gc_ref, wc_ref))
    o_ref[...] = x_ref[...] + jnp.dot(merged.astype(BF16), wo_ref[...], preferred_element_type=F32)


def _merge(x, ya, yb, yc, gates, wa, wb, wc, wo, tm):
    T, D = x.shape
    row = lambda i: (i, 0)
    return pl.pallas_call(
        _merge_kernel,
        out_shape=jax.ShapeDtypeStruct((T, D), F32),
        grid=(T // tm,),
        in_specs=[pl.BlockSpec((tm, D), row),
                  pl.BlockSpec((tm, ya.shape[1]), row),
                  pl.BlockSpec((tm, yb.shape[1]), row),
                  pl.BlockSpec((tm, yc.shape[1]), row),
                  pl.BlockSpec((tm, D), lambda i: (i, 0)),
                  pl.BlockSpec((tm, D), lambda i: (i, 1)),
                  pl.BlockSpec((tm, D), lambda i: (i, 2)),
                  _resident(wa.shape), _resident(wb.shape), _resident(wc.shape), _resident(wo.shape)],
        out_specs=pl.BlockSpec((tm, D), row),
        compiler_params=_params("parallel"),
        name="merge",
    )(x, ya, yb, yc, gates, gates, gates, wa, wb, wc, wo)


def _cross_kernel(x_ref, g_ref, kv_ref, wq_ref, wo_ref, o_ref):
    x = x_ref[...]
    h = _rms(x, g_ref[...]).astype(BF16)
    q = jnp.dot(h, wq_ref[...], preferred_element_type=F32).astype(BF16)
    kv = kv_ref[0]
    width = X_HEADS * X_DH
    outs = []
    for hd in range(X_HEADS):
        sl = slice(hd * X_DH, (hd + 1) * X_DH)
        k = kv[:, sl]
        v = kv[:, width + hd * X_DH: width + (hd + 1) * X_DH]
        s = lax.dot_general(q[:, sl], k, NT_DIMS, preferred_element_type=F32) * (X_DH ** -0.5)
        p = jnp.exp(s - jnp.max(s, axis=-1, keepdims=True))
        p = p / jnp.sum(p, axis=-1, keepdims=True)
        outs.append(jnp.dot(p.astype(BF16), v, preferred_element_type=F32).astype(BF16))
    o = jnp.concatenate(outs, axis=-1)
    o_ref[...] = x + jnp.dot(o, wo_ref[...], preferred_element_type=F32)


def _cross(x, g, kv, wq, wo, seq, tm):
    T, D = x.shape
    nseq = seq // tm
    row = lambda i: (i, 0)
    return pl.pallas_call(
        _cross_kernel,
        out_shape=jax.ShapeDtypeStruct((T, D), F32),
        grid=(T // tm,),
        in_specs=[pl.BlockSpec((tm, D), row),
                  _resident((1, D)),
                  pl.BlockSpec((1,) + kv.shape[1:], lambda i: (i // nseq, 0, 0)),
                  _resident(wq.shape), _resident(wo.shape)],
        out_specs=pl.BlockSpec((tm, D), row),
        compiler_params=_params("parallel"),
        name="cross_attn",
    )(x, g.reshape(1, D), kv, wq, wo)


def _swiglu_kernel(x_ref, g_ref, w1_ref, w3_ref, w2_ref, gf_ref, o_ref, *, final_norm):
    x = x_ref[...]
    h = _rms(x, g_ref[...]).astype(BF16)
    a = jnp.dot(h, w1_ref[...], preferred_element_type=F32)
    b = jnp.dot(h, w3_ref[...], preferred_element_type=F32)
    u = (jax.nn.silu(a) * b).astype(BF16)
    y = x + jnp.dot(u, w2_ref[...], preferred_element_type=F32)
    if final_norm:
        y = _rms(y, gf_ref[...])
    o_ref[...] = y


def _swiglu(x, g, w1, w3, w2, g_final, final_norm, tm):
    T, D = x.shape
    row = lambda i: (i, 0)
    return pl.pallas_call(
        functools.partial(_swiglu_kernel, final_norm=final_norm),
        out_shape=jax.ShapeDtypeStruct((T, D), F32),
        grid=(T // tm,),
        in_specs=[pl.BlockSpec((tm, D), row), _resident((1, D)),
                  _resident(w1.shape), _resident(w3.shape), _resident(w2.shape), _resident((1, D))],
        out_specs=pl.BlockSpec((tm, D), row),
        compiler_params=_params("parallel"),
        name="swiglu",
    )(x, g.reshape(1, D), w1, w3, w2, g_final.reshape(1, D))


def _pad_cols(w, width):
    return jnp.pad(w, ((0, 0), (0, width - w.shape[1])))


def _group_cols(w, heads, width):
    K = w.shape[0]
    w = w.reshape(K, heads, -1)
    return jnp.pad(w, ((0, 0), (0, 0), (0, width - w.shape[2]))).reshape(K, heads * width)


def _group_rows(w, heads, width):
    N = w.shape[1]
    w = w.reshape(heads, -1, N)
    return jnp.pad(w, ((0, 0), (0, width - w.shape[1]), (0, 0))).reshape(heads * width, N)


def _split_in_proj(w):
    o = 0
    parts = []
    for n in (A_Q_RANK, A_KV_RANK, A_ROPE, 3 * B_HEADS * B_DK + 2 * B_HEADS * B_DV,
              C_HEADS * C_DH, C_KV_HEADS * C_DH, C_KV_HEADS * C_DH, 3 * w.shape[0]):
        parts.append(w[:, o:o + n])
        o += n
    cq, ckv, kr, wb, wcq, wck, wcv, wg = parts
    half = A_ROPE // 2
    wa = jnp.concatenate([cq, ckv, _pad_cols(kr[:, :half], LANES), _pad_cols(kr[:, half:], LANES)], axis=1)
    wc = jnp.concatenate([_group_cols(wcq, C_HEADS, LANES), _group_cols(wck, C_KV_HEADS, LANES),
                          _group_cols(wcv, C_KV_HEADS, LANES)], axis=1)
    return [t.astype(BF16) for t in (wa, wb, wc, wg)]


def _mla_weights(wuq, wukv):
    half = A_ROPE // 2
    wq = wuq.reshape(A_Q_RANK, A_HEADS, A_NOPE + A_ROPE)
    wqn = _group_cols(wq[:, :, :A_NOPE].reshape(A_Q_RANK, -1), A_HEADS, LANES)
    wqr = jnp.concatenate([wq[:, :, A_NOPE:A_NOPE + half].reshape(A_Q_RANK, -1),
                           wq[:, :, A_NOPE + half:].reshape(A_Q_RANK, -1)], axis=1)
    wkv = wukv.reshape(A_KV_RANK, A_HEADS, A_NOPE + A_V)
    wkn = _group_cols(wkv[:, :, :A_NOPE].reshape(A_KV_RANK, -1), A_HEADS, LANES)
    wv = _group_cols(wkv[:, :, A_NOPE:].reshape(A_KV_RANK, -1), A_HEADS, LANES)
    return [t.astype(BF16) for t in (wqn, wqr, wkn, wv)]


def _mla_placement():
    half = A_ROPE // 2
    src = jnp.arange(LANES)[:, None]
    dst = jnp.arange(A_HEADS * LANES)[None, :]
    head, lane = dst // LANES, dst % LANES
    mats = []
    for per_head, base in ((True, A_NOPE), (True, A_NOPE + half), (False, A_NOPE), (False, A_NOPE + half)):
        j = lane - base
        want = head * half + j if per_head else j
        mats.append(((j >= 0) & (j < half) & (src == want)).astype(BF16))
    ones_row = (lane == A_V).astype(F32)
    return jnp.stack(mats), ones_row


def kernel(x, mem, w_in, g_mix, a_gq, a_gkv, a_wuq, a_wukv, b_lb, b_gout, c_sink, rel_bias,
           w_br_a, w_br_b, w_br_c, w_out, g_x, g_mem, x_wq, x_wkv, x_wo, g_ffn,
           f_w1, f_w3, f_w2, g_final):
    Bsz, S, D = x.shape
    depth = w_in.shape[0]
    T = Bsz * S
    M = mem.shape[1]
    tm = ROW_TILE
    half = A_ROPE // 2

    inv = ROPE_THETA ** (-jnp.arange(half, dtype=F32) / half)
    ang = jnp.arange(S, dtype=jnp.int32).astype(F32)[:, None] * inv[None, :]
    cos = jnp.tile(jnp.cos(ang), (1, LANES // half))
    sin = jnp.tile(jnp.sin(ang), (1, LANES // half))
    place, ones_row = _mla_placement()

    sm = jax.nn.softmax(b_lb.astype(F32), axis=1)
    lower = jnp.cumsum(sm, axis=1) - sm[:, :1]

    span = 3 * C_BLOCK
    rel = jnp.arange(span)[None, :] - C_BLOCK - jnp.arange(C_BLOCK)[:, None]
    G = C_HEADS // C_KV_HEADS
    onehot = (_t5_bucket(rel)[None] == jnp.arange(REL_BUCKETS)[:, None, None]).astype(F32)
    bias = jnp.einsum('nh,nqs->hqs', rel_bias.astype(F32), onehot,
                      precision=lax.Precision.HIGHEST).reshape(C_KV_HEADS, G, C_BLOCK, span)

    xt = x.reshape(T, D)
    mem2 = mem.reshape(Bsz * M, D)
    for l in range(depth):
        za, zb, zc, zg = _rms_proj(xt, g_mix[l], _split_in_proj(w_in[l]), (F32, F32, BF16, F32), IN_PROJ_TILE)

        qa, ka, va = _mla_prep(za, a_gq[l], a_gkv[l], *_mla_weights(a_wuq[l], a_wukv[l]),
                               place, ones_row, cos, sin, S, tm)
        shp = (Bsz, S, A_HEADS * LANES)
        ya = _mla_attn(qa.reshape(shp), ka.reshape(shp), va.reshape(shp),
                       MLA_TQ, MLA_SUB, MLA_STEPS).reshape(T, -1)

        of = _hgrn_direction(zb, lower[0, l], None, None, S, reverse=False)
        yb = _hgrn_direction(zb, lower[1, l], of, b_gout[l], S, reverse=True)

        yc = _window_gqa(zc, bias, c_sink[l].astype(F32), S)

        xt = _merge(xt, ya, yb, yc, zg,
                    _group_rows(w_br_a[l], A_HEADS, LANES).astype(BF16), w_br_b[l].astype(BF16),
                    _group_rows(w_br_c[l], C_HEADS, LANES).astype(BF16), w_out[l].astype(BF16), tm)

        (kvm,) = _rms_proj(mem2, g_mem[l], [x_wkv[l].astype(BF16)], (BF16,), Bsz * M)
        xt = _cross(xt, g_x[l], kvm.reshape(Bsz, M, -1), x_wq[l].astype(BF16), x_wo[l].astype(BF16), S, tm)

        xt = _swiglu(xt, g_ffn[l], f_w1[l].astype(BF16), f_w3[l].astype(BF16), f_w2[l].astype(BF16),
                     g_final, l == depth - 1, tm)
    return xt.reshape(Bsz, S, D)
```

```python
import functools
import math

import jax
import jax.numpy as jnp
from jax import lax
from jax.experimental import pallas as pl
from jax.experimental.pallas import tpu as pltpu

F32 = jnp.float32
BF16 = jnp.bfloat16

EPS = 1e-6
MASK_VALUE = -1e30
TINY = 1e-30
ROPE_THETA = 10000.0

A_HEADS, A_NOPE, A_ROPE, A_V = 8, 64, 32, 64
A_Q_RANK, A_KV_RANK = 384, 256
B_HEADS, B_DK, B_DV = 8, 128, 64
C_HEADS, C_KV_HEADS, C_DH, C_WINDOW, C_BLOCK = 8, 2, 64, 128, 128
REL_BUCKETS, REL_MAX_DIST = 32, 128
X_HEADS, X_DH = 4, 256

LANES = 128
VMEM_LIMIT = 56 * 1024 * 1024

ROW_TILE = 512
IN_PROJ_TILE = 256
MLA_TILING = ((1024, 256, 32), (512, 512, 16))
MLA_SLAB = 16
MLA_VROWS = 80
HGRN_CHUNK = 32
HGRN_BLOCK = 512

NT_DIMS = (((1,), (1,)), ((), ()))


def _params(*sem):
    return pltpu.CompilerParams(dimension_semantics=sem, vmem_limit_bytes=VMEM_LIMIT)


def _rms(x, g):
    return x * lax.rsqrt(jnp.mean(x * x, axis=-1, keepdims=True) + EPS) * g


def _resident(shape):
    return pl.BlockSpec(shape, lambda *_: (0,) * len(shape), pipeline_mode=pl.Buffered(1))


def _rms_proj_kernel(*refs, n_out):
    x_ref, g_ref = refs[:2]
    w_refs = refs[2:2 + n_out]
    o_refs = refs[2 + n_out:]
    h = _rms(x_ref[...], g_ref[...]).astype(BF16)
    for w_ref, o_ref in zip(w_refs, o_refs):
        o_ref[...] = jnp.dot(h, w_ref[...], preferred_element_type=F32).astype(o_ref.dtype)


def _rms_proj(x, g, weights, out_dtypes, tm):
    T, D = x.shape
    row = lambda i: (i, 0)
    return pl.pallas_call(
        functools.partial(_rms_proj_kernel, n_out=len(weights)),
        out_shape=tuple(jax.ShapeDtypeStruct((T, w.shape[1]), dt) for w, dt in zip(weights, out_dtypes)),
        grid=(T // tm,),
        in_specs=[pl.BlockSpec((tm, D), row), _resident((1, D))] + [_resident(w.shape) for w in weights],
        out_specs=tuple(pl.BlockSpec((tm, w.shape[1]), row) for w in weights),
        compiler_params=_params("parallel"),
        name="rms_proj",
    )(x, g.reshape(1, D), *weights)


def _mla_prep_kernel(za_ref, gq_ref, gkv_ref, wqn_ref, wqr_ref, wkn_ref, wv_ref, place_ref, ones_ref,
                     cos_ref, sin_ref, q_ref, k_ref, v_ref, *, q_scale):
    za = za_ref[...]
    cq = za[:, :A_Q_RANK]
    ckv = za[:, A_Q_RANK:A_Q_RANK + A_KV_RANK]
    kr1 = za[:, 640:768]
    kr2 = za[:, 768:896]
    cos = cos_ref[...]
    sin = sin_ref[...]

    def place(x, idx):
        return jnp.dot(x.astype(BF16), place_ref[idx], preferred_element_type=F32)

    hq = _rms(cq, gq_ref[...]).astype(BF16)
    qr = jnp.dot(hq, wqr_ref[...], preferred_element_type=F32)
    q1 = qr[:, :LANES]
    q2 = qr[:, LANES:]
    qn = jnp.dot(hq, wqn_ref[...], preferred_element_type=F32)
    q_ref[...] = (qn * q_scale + place((q1 * cos - q2 * sin) * q_scale, 0)
                  + place((q1 * sin + q2 * cos) * q_scale, 1)).astype(BF16)

    hkv = _rms(ckv, gkv_ref[...]).astype(BF16)
    kn = jnp.dot(hkv, wkn_ref[...], preferred_element_type=F32)
    k_ref[...] = (kn + place(kr1 * cos - kr2 * sin, 2) + place(kr1 * sin + kr2 * cos, 3)).astype(BF16)
    v_ref[...] = (jnp.dot(hkv, wv_ref[...], preferred_element_type=F32) + ones_ref[...]).astype(BF16)


def _mla_prep(za, gq, gkv, wqn, wqr, wkn, wv, place, ones_row, cos, sin, seq, tm):
    T = za.shape[0]
    width = A_HEADS * LANES
    nseq = seq // tm
    q_scale = (A_NOPE + A_ROPE) ** -0.5 * math.log2(math.e)
    row = lambda i: (i, 0)
    pos = lambda i: (i % nseq, 0)
    out = jax.ShapeDtypeStruct((T, width), BF16)
    return pl.pallas_call(
        functools.partial(_mla_prep_kernel, q_scale=q_scale),
        out_shape=(out, out, out),
        grid=(T // tm,),
        in_specs=[pl.BlockSpec((tm, za.shape[1]), row),
                  _resident((1, A_Q_RANK)), _resident((1, A_KV_RANK)),
                  _resident(wqn.shape), _resident(wqr.shape), _resident(wkn.shape), _resident(wv.shape),
                  _resident(place.shape), _resident(ones_row.shape),
                  pl.BlockSpec((tm, LANES), pos), pl.BlockSpec((tm, LANES), pos)],
        out_specs=(pl.BlockSpec((tm, width), row),) * 3,
        compiler_params=_params("parallel"),
        name="mla_prep",
    )(za, gq.reshape(1, -1), gkv.reshape(1, -1), wqn, wqr, wkn, wv, place, ones_row, cos, sin)


def _mla_attn_kernel(q_ref, k_ref, v_ref, o_ref, vt_ref, s_buf, p_buf, *, sub, steps):
    tq = q_ref.shape[1]
    seq = k_ref.shape[1]
    n_sub = seq // sub

    @pl.when(pl.program_id(2) == 0)
    def _():
        def tr(c, carry):
            off = pl.multiple_of(c * sub, sub)
            vt = v_ref[0, pl.ds(off, sub), :].astype(F32).T
            vt_ref[:, pl.ds(off, sub)] = vt[:MLA_VROWS].astype(BF16)
            return carry
        lax.fori_loop(0, n_sub, tr, 0)

    qt = q_ref[0].astype(F32).T.astype(BF16)

    def qk(j, slot):
        off = pl.multiple_of(jnp.minimum(j, n_sub - 1) * sub, sub)
        s = jnp.dot(k_ref[0, pl.ds(off, sub), :], qt, preferred_element_type=F32)
        s_buf[slot] = s
        return jnp.max(s, axis=0, keepdims=True)

    def step(j, slot, m, acc, cmax):
        cmax_next = qk(j + 1, 1 - slot)
        m_new = jnp.maximum(m, cmax)
        alpha = jnp.exp2(m - m_new)
        mb = jnp.broadcast_to(m_new, (MLA_SLAB, tq))
        for r in range(sub // MLA_SLAB):
            rows = pl.ds(r * MLA_SLAB, MLA_SLAB)
            p_buf[slot, rows, :] = jnp.exp2(s_buf[slot, rows, :] - mb).astype(BF16)
        off = pl.multiple_of(j * sub, sub)
        acc = alpha * acc + jnp.dot(vt_ref[:, pl.ds(off, sub)], p_buf[slot], preferred_element_type=F32)
        return m_new, acc, cmax_next

    def body(i, carry):
        for u in range(steps):
            carry = step(steps * i + u, u % 2, *carry)
        return carry

    m0 = jnp.full((1, tq), MASK_VALUE, F32)
    acc0 = jnp.zeros((MLA_VROWS, tq), F32)
    _, acc, _ = lax.fori_loop(0, n_sub // steps, body, (m0, acc0, qk(0, 0)))
    ot = acc[:A_V] / acc[A_V:A_V + 1]
    ot = jnp.concatenate([ot, jnp.zeros((LANES - A_V, tq), F32)], axis=0)
    o_ref[0] = ot.T.astype(o_ref.dtype)


def _mla_attn(q, k, v, tq, sub, steps):
    Bsz, S, width = q.shape
    assert S % tq == 0 and S % (sub * steps) == 0 and steps % 2 == 0
    return pl.pallas_call(
        functools.partial(_mla_attn_kernel, sub=sub, steps=steps),
        out_shape=jax.ShapeDtypeStruct((Bsz, S, width), BF16),
        grid=(Bsz, width // LANES, S // tq),
        in_specs=[pl.BlockSpec((1, tq, LANES), lambda b, h, i: (b, i, h)),
                  pl.BlockSpec((1, S, LANES), lambda b, h, i: (b, 0, h)),
                  pl.BlockSpec((1, S, LANES), lambda b, h, i: (b, 0, h))],
        out_specs=pl.BlockSpec((1, tq, LANES), lambda b, h, i: (b, i, h)),
        scratch_shapes=[pltpu.VMEM((MLA_VROWS, S), BF16),
                        pltpu.VMEM((2, sub, tq), F32), pltpu.VMEM((2, sub, tq), BF16)],
        compiler_params=_params("parallel", "parallel", "arbitrary"),
        name="mla_attn",
    )(q, k, v)


def _chunk_cumsum(x, chunk, reverse):
    n, w = x.shape
    sub = 8
    per = chunk // sub
    x3 = x.reshape(n // sub, sub, w)
    row = lax.broadcasted_iota(jnp.int32, x3.shape, 1)
    d = 1
    while d < sub:
        if reverse:
            x3 = x3 + jnp.where(row < sub - d, pltpu.roll(x3, sub - d, 1), 0.0)
        else:
            x3 = x3 + jnp.where(row >= d, pltpu.roll(x3, d, 1), 0.0)
        d *= 2
    xc = x3.reshape(n // chunk, chunk, w)
    parts = [xc[:, i * sub:(i + 1) * sub, :] for i in range(per)]
    edge = 0 if reverse else sub - 1
    carry = None
    for i in (range(per - 1, -1, -1) if reverse else range(per)):
        total = parts[i][:, edge:edge + 1, :]
        if carry is not None:
            parts[i] = parts[i] + carry
            carry = carry + total
        else:
            carry = total
    return jnp.concatenate(parts, axis=1).reshape(n, w)


def _hgrn_kernel(*refs, reverse, finalize):
    if finalize:
        (q_ref, z_ref, i_ref, lb_ref, prev_ref, og_ref, gout_ref, o_ref, st_ref, sall_ref) = refs
    else:
        (q_ref, z_ref, i_ref, lb_ref, o_ref, st_ref, sall_ref) = refs
    C = HGRN_CHUNK
    tb = q_ref.shape[0]
    nc = tb // C
    half = C // 2

    @pl.when(pl.program_id(2) == 0)
    def _():
        st_ref[...] = jnp.zeros_like(st_ref)

    q = q_ref[...]
    z = z_ref[...]
    lb = lb_ref[...]
    v = i_ref[...]
    sg = jax.nn.sigmoid(z)
    f = lb + (1.0 - lb) * sg
    key = (1.0 - lb) * (1.0 - sg)
    b = _chunk_cumsum(jnp.log(jnp.maximum(f, TINY)), C, reverse)

    b3 = b.reshape(nc, C, 2 * B_DK)
    if reverse:
        ref3 = b3[:, half:half + 1, :]
        tot3 = b3[:, 0:1, :]
    else:
        ref3 = b3[:, half - 1:half, :]
        tot3 = b3[:, C - 1:C, :]
    qs = q.reshape(nc, C, 2 * B_DK) * jnp.exp(b3 - ref3)
    ks = key.reshape(nc, C, 2 * B_DK) * jnp.exp(ref3 - b3)
    q_dec = (qs * jnp.exp(ref3)).astype(BF16)
    k_end = (ks * jnp.exp(tot3 - ref3)).astype(BF16)
    dec = jnp.exp(tot3)
    qs = qs.astype(BF16)
    ks = ks.astype(BF16)
    v3 = v.astype(BF16).reshape(nc, C, 2 * B_DV)

    t_idx = lax.broadcasted_iota(jnp.int32, (nc, C, C), 1)
    s_idx = lax.broadcasted_iota(jnp.int32, (nc, C, C), 2)
    causal = (s_idx >= t_idx) if reverse else (s_idx <= t_idx)

    outs = []
    for h in range(2):
        sl = slice(h * B_DK, (h + 1) * B_DK)
        scores = jnp.einsum('ctk,csk->cts', qs[:, :, sl], ks[:, :, sl], preferred_element_type=F32)
        scores = jnp.where(causal, scores, 0.0).astype(BF16)
        upd = jnp.einsum('csv,csk->cvk', v3, k_end[:, :, sl], preferred_element_type=F32)
        order = range(nc - 1, -1, -1) if reverse else range(nc)
        state = st_ref[h]
        for c in order:
            sall_ref[h, c] = state.astype(BF16)
            state = state * dec[c, :, sl] + upd[c]
        st_ref[h] = state
        o_h = (jnp.einsum('cts,csv->ctv', scores, v3, preferred_element_type=F32)
               + jnp.einsum('ctk,cvk->ctv', q_dec[:, :, sl], sall_ref[h], preferred_element_type=F32))
        outs.append(o_h.reshape(tb, 2 * B_DV))
    lane = lax.broadcasted_iota(jnp.int32, (tb, 2 * B_DV), 1)
    first = lane < B_DV
    o = jnp.where(first, outs[0], outs[1])
    if not finalize:
        o_ref[...] = o
        return
    o = o + prev_ref[...]
    sq = o * o
    ms0 = jnp.sum(jnp.where(first, sq, 0.0), axis=-1, keepdims=True)
    ms1 = jnp.sum(jnp.where(first, 0.0, sq), axis=-1, keepdims=True)
    ms = jnp.where(first, ms0, ms1) * (1.0 / B_DV)
    y = o * lax.rsqrt(ms + EPS) * gout_ref[...]
    o_ref[...] = (y * jax.nn.silu(og_ref[...])).astype(o_ref.dtype)


def _hgrn_direction(zb, lb, prev, gout, seq, reverse):
    T = zb.shape[0]
    Bsz = T // seq
    tb = HGRN_BLOCK
    nblk = seq // tb
    pairs = B_HEADS // 2
    dk2, dv2 = 2 * B_DK, 2 * B_DV
    finalize = prev is not None

    def row(b, p, i):
        return b * nblk + (nblk - 1 - i if reverse else i)

    z_base = (2 if reverse else 1) * (B_HEADS * B_DK) // dk2
    i_base = 3 * (B_HEADS * B_DK) // dv2
    g_base = i_base + (B_HEADS * B_DV) // dv2
    in_specs = [pl.BlockSpec((tb, dk2), lambda b, p, i: (row(b, p, i), p)),
                pl.BlockSpec((tb, dk2), lambda b, p, i: (row(b, p, i), z_base + p)),
                pl.BlockSpec((tb, dv2), lambda b, p, i: (row(b, p, i), i_base + p)),
                pl.BlockSpec((1, dk2), lambda b, p, i: (0, p))]
    args = [zb, zb, zb, lb.reshape(1, -1)]
    if finalize:
        in_specs += [pl.BlockSpec((tb, dv2), lambda b, p, i: (row(b, p, i), p)),
                     pl.BlockSpec((tb, dv2), lambda b, p, i: (row(b, p, i), g_base + p)),
                     pl.BlockSpec((1, dv2), lambda b, p, i: (0, 0))]
        args += [prev, zb, jnp.tile(gout, 2).reshape(1, dv2)]
    return pl.pallas_call(
        functools.partial(_hgrn_kernel, reverse=reverse, finalize=finalize),
        out_shape=jax.ShapeDtypeStruct((T, B_HEADS * B_DV), BF16 if finalize else F32),
        grid=(Bsz, pairs, nblk),
        in_specs=in_specs,
        out_specs=pl.BlockSpec((tb, dv2), lambda b, p, i: (row(b, p, i), p)),
        scratch_shapes=[pltpu.VMEM((2, dv2, B_DK), F32),
                        pltpu.VMEM((2, tb // HGRN_CHUNK, dv2, B_DK), BF16)],
        compiler_params=_params("parallel", "parallel", "arbitrary"),
        name="hgrn_bwd" if reverse else "hgrn_fwd",
    )(*args)


def _t5_bucket(rel):
    nb = REL_BUCKETS // 2
    max_exact = nb // 2
    ret = (rel > 0).astype(jnp.int32) * nb
    n = jnp.abs(rel)
    large = max_exact + (jnp.log(jnp.maximum(n, 1).astype(F32) / max_exact)
                         / math.log(REL_MAX_DIST / max_exact) * (nb - max_exact)).astype(jnp.int32)
    large = jnp.minimum(large, nb - 1)
    return ret + jnp.where(n < max_exact, n, large)


def _win_kernel(sink_ref, q_ref, kp_ref, kc_ref, kn_ref, vp_ref, vc_ref, vn_ref, bias_ref, mask_ref, o_ref):
    G = C_HEADS // C_KV_HEADS
    blk = C_BLOCK
    n = pl.program_id(1)
    last = pl.num_programs(1) - 1
    rows, span = G * blk, 3 * blk

    which = jnp.where(n == 0, 1, jnp.where(n == last, 2, 0))
    valid = mask_ref[which] != 0.0
    g_idx = lax.broadcasted_iota(jnp.int32, (rows, 1), 0) // blk

    for kvh in range(C_KV_HEADS):
        grp = slice(kvh * LANES, (kvh + 1) * LANES)
        q = jnp.concatenate([q_ref[:, (kvh * G + g) * LANES:(kvh * G + g + 1) * LANES] for g in range(G)],
                            axis=0)
        k = jnp.concatenate([kp_ref[:, grp], kc_ref[:, grp], kn_ref[:, grp]], axis=0)
        v = jnp.concatenate([vp_ref[:, grp], vc_ref[:, grp], vn_ref[:, grp]], axis=0)
        s = lax.dot_general(q, k, NT_DIMS, preferred_element_type=F32) * (C_DH ** -0.5)
        s = s + bias_ref[kvh].reshape(rows, span)
        s = jnp.where(valid, s, MASK_VALUE)
        sink = jnp.zeros((rows, 1), F32)
        for g in range(G):
            sink = jnp.where(g_idx == g, sink_ref[kvh * G + g], sink)
        m = jnp.maximum(jnp.max(s, axis=-1, keepdims=True), sink)
        p = jnp.exp(s - m)
        p = p / (jnp.sum(p, axis=-1, keepdims=True) + jnp.exp(sink - m))
        o = jnp.dot(p.astype(BF16), v, preferred_element_type=F32)
        for pair in range(G // 2):
            first = o[(2 * pair) * blk:(2 * pair + 1) * blk]
            second = pltpu.roll(o[(2 * pair + 1) * blk:(2 * pair + 2) * blk], C_DH, 1)
            col = (kvh * G // 2 + pair) * LANES
            o_ref[:, col:col + LANES] = (first + second).astype(o_ref.dtype)


def _window_gqa(zc, bias, mask, sink, seq):
    T = zc.shape[0]
    blk = C_BLOCK
    nb = seq // blk
    assert nb >= 2
    qw = C_HEADS * LANES
    kw = C_KV_HEADS * LANES
    G = C_HEADS // C_KV_HEADS

    def kv_spec(col, shift):
        return pl.BlockSpec((blk, kw), lambda b, n, sref: (b * nb + jnp.clip(n + shift, 0, nb - 1), col))

    k_col, v_col = qw // kw, qw // kw + 1
    grid_spec = pltpu.PrefetchScalarGridSpec(
        num_scalar_prefetch=1,
        grid=(T // seq, nb),
        in_specs=[pl.BlockSpec((blk, qw), lambda b, n, sref: (b * nb + n, 0)),
                  kv_spec(k_col, -1), kv_spec(k_col, 0), kv_spec(k_col, 1),
                  kv_spec(v_col, -1), kv_spec(v_col, 0), kv_spec(v_col, 1),
                  pl.BlockSpec((C_KV_HEADS, G, blk, 3 * blk), lambda b, n, sref: (0, 0, 0, 0)),
                  pl.BlockSpec(mask.shape, lambda b, n, sref: (0, 0, 0))],
        out_specs=pl.BlockSpec((blk, C_HEADS * C_DH), lambda b, n, sref: (b * nb + n, 0)))
    return pl.pallas_call(
        _win_kernel,
        out_shape=jax.ShapeDtypeStruct((T, C_HEADS * C_DH), BF16),
        grid_spec=grid_spec,
        compiler_params=_params("parallel", "arbitrary"),
        name="window_gqa",
    )(sink, zc, zc, zc, zc, zc, zc, zc, bias, mask)


def _merge_kernel(x_ref, ya_ref, yb_ref, yc_ref, ga_ref, gb_ref, gc_ref,
                  wa_ref, wb_ref, wc_ref, wo_ref, o_ref):
    def branch(y_ref, gate_ref, w_ref):
        return jax.nn.sigmoid(gate_ref[...]) * jnp.dot(y_ref[...], w_ref[...], preferred_element_type=F32)

    merged = (branch(ya_ref, ga_ref, wa_ref) + branch(yb_ref, gb_ref, wb_ref)
              + branch(yc_ref, gc_ref, wc_ref))
    o_ref[...] = x_ref[...] + jnp.dot(merged.astype(BF16), wo_ref[...], preferred_element_type=F32)


def _merge(x, ya, yb, yc, gates, wa, wb, wc, wo, tm):
    T, D = x.shape
    row = lambda i: (i, 0)
    return pl.pallas_call(
        _merge_kernel,
        out_shape=jax.ShapeDtypeStruct((T, D), F32),
        grid=(T // tm,),
        in_specs=[pl.BlockSpec((tm, D), row),
                  pl.BlockSpec((tm, ya.shape[1]), row),
                  pl.BlockSpec((tm, yb.shape[1]), row),
                  pl.BlockSpec((tm, yc.shape[1]), row),
                  pl.BlockSpec((tm, D), lambda i: (i, 0)),
                  pl.BlockSpec((tm, D), lambda i: (i, 1)),
                  pl.BlockSpec((tm, D), lambda i: (i, 2)),
                  _resident(wa.shape), _resident(wb.shape), _resident(wc.shape), _resident(wo.shape)],
        out_specs=pl.BlockSpec((tm, D), row),
        compiler_params=_params("parallel"),
        name="merge",
    )(x, ya, yb, yc, gates, gates, gates, wa, wb, wc, wo)


def _cross_kernel(x_ref, g_ref, kv_ref, wq_ref, wo_ref, o_ref):
    x = x_ref[...]
    h = _rms(x, g_ref[...]).astype(BF16)
    q = jnp.dot(h, wq_ref[...], preferred_element_type=F32).astype(BF16)
    kv = kv_ref[0]
    width = X_HEADS * X_DH
    outs = []
    for hd in range(X_HEADS):
        sl = slice(hd * X_DH, (hd + 1) * X_DH)
        k = kv[:, sl]
        v = kv[:, width + hd * X_DH: width + (hd + 1) * X_DH]
        s = lax.dot_general(q[:, sl], k, NT_DIMS, preferred_element_type=F32) * (X_DH ** -0.5)
        p = jnp.exp(s - jnp.max(s, axis=-1, keepdims=True))
        p = p / jnp.sum(p, axis=-1, keepdims=True)
        outs.append(jnp.dot(p.astype(BF16), v, preferred_element_type=F32).astype(BF16))
    o = jnp.concatenate(outs, axis=-1)
    o_ref[...] = x + jnp.dot(o, wo_ref[...], preferred_element_type=F32)


def _cross(x, g, kv, wq, wo, seq, tm):
    T, D = x.shape
    nseq = seq // tm
    row = lambda i: (i, 0)
    return pl.pallas_call(
        _cross_kernel,
        out_shape=jax.ShapeDtypeStruct((T, D), F32),
        grid=(T // tm,),
        in_specs=[pl.BlockSpec((tm, D), row),
                  _resident((1, D)),
                  pl.BlockSpec((1,) + kv.shape[1:], lambda i: (i // nseq, 0, 0)),
                  _resident(wq.shape), _resident(wo.shape)],
        out_specs=pl.BlockSpec((tm, D), row),
        compiler_params=_params("parallel"),
        name="cross_attn",
    )(x, g.reshape(1, D), kv, wq, wo)


def _swiglu_kernel(x_ref, g_ref, w1_ref, w3_ref, w2_ref, gf_ref, o_ref, *, final_norm):
    x = x_ref[...]
    h = _rms(x, g_ref[...]).astype(BF16)
    a = jnp.dot(h, w1_ref[...], preferred_element_type=F32)
    b = jnp.dot(h, w3_ref[...], preferred_element_type=F32)
    u = (jax.nn.silu(a) * b).astype(BF16)
    y = x + jnp.dot(u, w2_ref[...], preferred_element_type=F32)
    if final_norm:
        y = _rms(y, gf_ref[...])
    o_ref[...] = y


def _swiglu(x, g, w1, w3, w2, g_final, final_norm, tm):
    T, D = x.shape
    row = lambda i: (i, 0)
    return pl.pallas_call(
        functools.partial(_swiglu_kernel, final_norm=final_norm),
        out_shape=jax.ShapeDtypeStruct((T, D), F32),
        grid=(T // tm,),
        in_specs=[pl.BlockSpec((tm, D), row), _resident((1, D)),
                  _resident(w1.shape), _resident(w3.shape), _resident(w2.shape), _resident((1, D))],
        out_specs=pl.BlockSpec((tm, D), row),
        compiler_params=_params("parallel"),
        name="swiglu",
    )(x, g.reshape(1, D), w1, w3, w2, g_final.reshape(1, D))


def _pad_cols(w, width):
    return jnp.pad(w, ((0, 0), (0, width - w.shape[1])))


def _group_cols(w, heads, width):
    K = w.shape[0]
    w = w.reshape(K, heads, -1)
    return jnp.pad(w, ((0, 0), (0, 0), (0, width - w.shape[2]))).reshape(K, heads * width)


def _group_rows(w, heads, width):
    N = w.shape[1]
    w = w.reshape(heads, -1, N)
    return jnp.pad(w, ((0, 0), (0, width - w.shape[1]), (0, 0))).reshape(heads * width, N)


def _split_in_proj(w):
    o = 0
    parts = []
    for n in (A_Q_RANK, A_KV_RANK, A_ROPE, 3 * B_HEADS * B_DK + 2 * B_HEADS * B_DV,
              C_HEADS * C_DH, C_KV_HEADS * C_DH, C_KV_HEADS * C_DH, 3 * w.shape[0]):
        parts.append(w[:, o:o + n])
        o += n
    cq, ckv, kr, wb, wcq, wck, wcv, wg = parts
    half = A_ROPE // 2
    wa = jnp.concatenate([cq, ckv, _pad_cols(kr[:, :half], LANES), _pad_cols(kr[:, half:], LANES)], axis=1)
    wc = jnp.concatenate([_group_cols(wcq, C_HEADS, LANES), _group_cols(wck, C_KV_HEADS, LANES),
                          _group_cols(wcv, C_KV_HEADS, LANES)], axis=1)
    return [t.astype(BF16) for t in (wa, wb, wc, wg)]


def _mla_weights(wuq, wukv):
    half = A_ROPE // 2
    wq = wuq.reshape(A_Q_RANK, A_HEADS, A_NOPE + A_ROPE)
    wqn = _group_cols(wq[:, :, :A_NOPE].reshape(A_Q_RANK, -1), A_HEADS, LANES)
    wqr = jnp.concatenate([wq[:, :, A_NOPE:A_NOPE + half].reshape(A_Q_RANK, -1),
                           wq[:, :, A_NOPE + half:].reshape(A_Q_RANK, -1)], axis=1)
    wkv = wukv.reshape(A_KV_RANK, A_HEADS, A_NOPE + A_V)
    wkn = _group_cols(wkv[:, :, :A_NOPE].reshape(A_KV_RANK, -1), A_HEADS, LANES)
    wv = _group_cols(wkv[:, :, A_NOPE:].reshape(A_KV_RANK, -1), A_HEADS, LANES)
    return [t.astype(BF16) for t in (wqn, wqr, wkn, wv)]


def _mla_placement():
    half = A_ROPE // 2
    src = jnp.arange(LANES)[:, None]
    dst = jnp.arange(A_HEADS * LANES)[None, :]
    head, lane = dst // LANES, dst % LANES
    mats = []
    for per_head, base in ((True, A_NOPE), (True, A_NOPE + half), (False, A_NOPE), (False, A_NOPE + half)):
        j = lane - base
        want = head * half + j if per_head else j
        mats.append(((j >= 0) & (j < half) & (src == want)).astype(BF16))
    ones_row = (lane == A_V).astype(F32)
    return jnp.stack(mats), ones_row


def kernel(x, mem, w_in, g_mix, a_gq, a_gkv, a_wuq, a_wukv, b_lb, b_gout, c_sink, rel_bias,
           w_br_a, w_br_b, w_br_c, w_out, g_x, g_mem, x_wq, x_wkv, x_wo, g_ffn,
           f_w1, f_w3, f_w2, g_final):
    Bsz, S, D = x.shape
    depth = w_in.shape[0]
    T = Bsz * S
    M = mem.shape[1]
    tm = ROW_TILE
    half = A_ROPE // 2

    inv = ROPE_THETA ** (-jnp.arange(half, dtype=F32) / half)
    ang = jnp.arange(S, dtype=jnp.int32).astype(F32)[:, None] * inv[None, :]
    cos = jnp.tile(jnp.cos(ang), (1, LANES // half))
    sin = jnp.tile(jnp.sin(ang), (1, LANES // half))
    place, ones_row = _mla_placement()

    sm = jax.nn.softmax(b_lb.astype(F32), axis=1)
    lower = jnp.cumsum(sm, axis=1) - sm[:, :1]

    span = 3 * C_BLOCK
    rel = jnp.arange(span)[None, :] - C_BLOCK - jnp.arange(C_BLOCK)[:, None]
    G = C_HEADS // C_KV_HEADS
    onehot = (_t5_bucket(rel)[None] == jnp.arange(REL_BUCKETS)[:, None, None]).astype(F32)
    bias = jnp.einsum('nh,nqs->hqs', rel_bias.astype(F32), onehot,
                      precision=lax.Precision.HIGHEST).reshape(C_KV_HEADS, G, C_BLOCK, span)
    col = jnp.arange(span)[None, :]
    band = jnp.abs(rel) <= C_WINDOW
    win_mask = jnp.stack([band, band & (col >= C_BLOCK), band & (col < 2 * C_BLOCK)]).astype(F32)
    win_mask = jnp.tile(win_mask, (1, G, 1))

    xt = x.reshape(T, D)
    mem2 = mem.reshape(Bsz * M, D)
    for l in range(depth):
        za, zb, zc, zg = _rms_proj(xt, g_mix[l], _split_in_proj(w_in[l]), (F32, F32, BF16, F32), IN_PROJ_TILE)

        qa, ka, va = _mla_prep(za, a_gq[l], a_gkv[l], *_mla_weights(a_wuq[l], a_wukv[l]),
                               place, ones_row, cos, sin, S, tm)
        shp = (Bsz, S, A_HEADS * LANES)
        ya = _mla_attn(qa.reshape(shp), ka.reshape(shp), va.reshape(shp), *MLA_TILING[l % 2]).reshape(T, -1)

        of = _hgrn_direction(zb, lower[0, l], None, None, S, reverse=False)
        yb = _hgrn_direction(zb, lower[1, l], of, b_gout[l], S, reverse=True)

        yc = _window_gqa(zc, bias, win_mask, c_sink[l].astype(F32), S)

        xt = _merge(xt, ya, yb, yc, zg,
                    _group_rows(w_br_a[l], A_HEADS, LANES).astype(BF16), w_br_b[l].astype(BF16),
                    w_br_c[l].astype(BF16), w_out[l].astype(BF16), tm)

        (kvm,) = _rms_proj(mem2, g_mem[l], [x_wkv[l].astype(BF16)], (BF16,), Bsz * M)
        xt = _cross(xt, g_x[l], kvm.reshape(Bsz, M, -1), x_wq[l].astype(BF16), x_wo[l].astype(BF16), S, tm)

        xt = _swiglu(xt, g_ffn[l], f_w1[l].astype(BF16), f_w3[l].astype(BF16), f_w2[l].astype(BF16),
                     g_final, l == depth - 1, tm)
    return xt.reshape(Bsz, S, D)
```

```python
import functools
import math

import jax
import jax.numpy as jnp
from jax import lax
from jax.experimental import pallas as pl
from jax.experimental.pallas import tpu as pltpu

F32 = jnp.float32
BF16 = jnp.bfloat16

EPS = 1e-6
MASK_VALUE = -1e30
TINY = 1e-30
ROPE_THETA = 10000.0

A_HEADS, A_NOPE, A_ROPE, A_V = 8, 64, 32, 64
A_Q_RANK, A_KV_RANK = 384, 256
B_HEADS, B_DK, B_DV = 8, 128, 64
C_HEADS, C_KV_HEADS, C_DH, C_WINDOW, C_BLOCK = 8, 2, 64, 128, 128
REL_BUCKETS, REL_MAX_DIST = 32, 128
X_HEADS, X_DH = 4, 256

LANES = 128
VMEM_LIMIT = 56 * 1024 * 1024

ROW_TILE = 512
IN_PROJ_TILE = 256
MLA_TQ = 1024
MLA_SUB = 256
MLA_STEPS = 32
MLA_SLAB = 16
MLA_VROWS = 80
HGRN_CHUNK = 32
HGRN_BLOCK = 1024

NT_DIMS = (((1,), (1,)), ((), ()))


def _params(*sem):
    return pltpu.CompilerParams(dimension_semantics=sem, vmem_limit_bytes=VMEM_LIMIT)


def _rms(x, g):
    return x * lax.rsqrt(jnp.mean(x * x, axis=-1, keepdims=True) + EPS) * g


def _resident(shape):
    return pl.BlockSpec(shape, lambda *_: (0,) * len(shape), pipeline_mode=pl.Buffered(1))


def _rms_proj_kernel(*refs, n_out):
    x_ref, g_ref = refs[:2]
    w_refs = refs[2:2 + n_out]
    o_refs = refs[2 + n_out:]
    h = _rms(x_ref[...], g_ref[...]).astype(BF16)
    for w_ref, o_ref in zip(w_refs, o_refs):
        o_ref[...] = jnp.dot(h, w_ref[...], preferred_element_type=F32).astype(o_ref.dtype)


def _rms_proj(x, g, weights, out_dtypes, tm):
    T, D = x.shape
    row = lambda i: (i, 0)
    return pl.pallas_call(
        functools.partial(_rms_proj_kernel, n_out=len(weights)),
        out_shape=tuple(jax.ShapeDtypeStruct((T, w.shape[1]), dt) for w, dt in zip(weights, out_dtypes)),
        grid=(T // tm,),
        in_specs=[pl.BlockSpec((tm, D), row), _resident((1, D))] + [_resident(w.shape) for w in weights],
        out_specs=tuple(pl.BlockSpec((tm, w.shape[1]), row) for w in weights),
        compiler_params=_params("parallel"),
        name="rms_proj",
    )(x, g.reshape(1, D), *weights)


def _mla_prep_kernel(za_ref, gq_ref, gkv_ref, wqn_ref, wqr_ref, wkn_ref, wv_ref, place_ref, ones_ref,
                     cos_ref, sin_ref, q_ref, k_ref, v_ref, *, q_scale):
    za = za_ref[...]
    cq = za[:, :A_Q_RANK]
    ckv = za[:, A_Q_RANK:A_Q_RANK + A_KV_RANK]
    kr1 = za[:, 640:768]
    kr2 = za[:, 768:896]
    cos = cos_ref[...]
    sin = sin_ref[...]

    def place(x, idx):
        return jnp.dot(x.astype(BF16), place_ref[idx], preferred_element_type=F32)

    hq = _rms(cq, gq_ref[...]).astype(BF16)
    qr = jnp.dot(hq, wqr_ref[...], preferred_element_type=F32)
    q1 = qr[:, :LANES]
    q2 = qr[:, LANES:]
    qn = jnp.dot(hq, wqn_ref[...], preferred_element_type=F32)
    q_ref[...] = (qn * q_scale + place((q1 * cos - q2 * sin) * q_scale, 0)
                  + place((q1 * sin + q2 * cos) * q_scale, 1)).astype(BF16)

    hkv = _rms(ckv, gkv_ref[...]).astype(BF16)
    kn = jnp.dot(hkv, wkn_ref[...], preferred_element_type=F32)
    k_ref[...] = (kn + place(kr1 * cos - kr2 * sin, 2) + place(kr1 * sin + kr2 * cos, 3)).astype(BF16)
    v_ref[...] = (jnp.dot(hkv, wv_ref[...], preferred_element_type=F32) + ones_ref[...]).astype(BF16)


def _mla_prep(za, gq, gkv, wqn, wqr, wkn, wv, place, ones_row, cos, sin, seq, tm):
    T = za.shape[0]
    width = A_HEADS * LANES
    nseq = seq // tm
    q_scale = (A_NOPE + A_ROPE) ** -0.5 * math.log2(math.e)
    row = lambda i: (i, 0)
    pos = lambda i: (i % nseq, 0)
    out = jax.ShapeDtypeStruct((T, width), BF16)
    return pl.pallas_call(
        functools.partial(_mla_prep_kernel, q_scale=q_scale),
        out_shape=(out, out, out),
        grid=(T // tm,),
        in_specs=[pl.BlockSpec((tm, za.shape[1]), row),
                  _resident((1, A_Q_RANK)), _resident((1, A_KV_RANK)),
                  _resident(wqn.shape), _resident(wqr.shape), _resident(wkn.shape), _resident(wv.shape),
                  _resident(place.shape), _resident(ones_row.shape),
                  pl.BlockSpec((tm, LANES), pos), pl.BlockSpec((tm, LANES), pos)],
        out_specs=(pl.BlockSpec((tm, width), row),) * 3,
        compiler_params=_params("parallel"),
        name="mla_prep",
    )(za, gq.reshape(1, -1), gkv.reshape(1, -1), wqn, wqr, wkn, wv, place, ones_row, cos, sin)


def _mla_attn_kernel(q_ref, k_ref, v_ref, o_ref, vt_ref, s_buf, p_buf, *, sub, steps):
    tq = q_ref.shape[1]
    seq = k_ref.shape[1]
    n_sub = seq // sub

    @pl.when(pl.program_id(2) == 0)
    def _():
        def tr(c, carry):
            off = pl.multiple_of(c * sub, sub)
            vt = v_ref[0, pl.ds(off, sub), :].astype(F32).T
            vt_ref[:, pl.ds(off, sub)] = vt[:MLA_VROWS].astype(BF16)
            return carry
        lax.fori_loop(0, n_sub, tr, 0)

    qt = q_ref[0].astype(F32).T.astype(BF16)

    def qk(j, slot):
        off = pl.multiple_of(jnp.minimum(j, n_sub - 1) * sub, sub)
        s = jnp.dot(k_ref[0, pl.ds(off, sub), :], qt, preferred_element_type=F32)
        s_buf[slot] = s
        return jnp.max(s, axis=0, keepdims=True)

    def step(j, slot, m, acc, cmax):
        cmax_next = qk(j + 1, 1 - slot)
        m_new = jnp.maximum(m, cmax)
        alpha = jnp.exp2(m - m_new)
        mb = jnp.broadcast_to(m_new, (MLA_SLAB, tq))
        for r in range(sub // MLA_SLAB):
            rows = pl.ds(r * MLA_SLAB, MLA_SLAB)
            p_buf[slot, rows, :] = jnp.exp2(s_buf[slot, rows, :] - mb).astype(BF16)
        off = pl.multiple_of(j * sub, sub)
        acc = alpha * acc + jnp.dot(vt_ref[:, pl.ds(off, sub)], p_buf[slot], preferred_element_type=F32)
        return m_new, acc, cmax_next

    def body(i, carry):
        for u in range(steps):
            carry = step(steps * i + u, u % 2, *carry)
        return carry

    m0 = jnp.full((1, tq), MASK_VALUE, F32)
    acc0 = jnp.zeros((MLA_VROWS, tq), F32)
    _, acc, _ = lax.fori_loop(0, n_sub // steps, body, (m0, acc0, qk(0, 0)))
    ot = acc[:A_V] / acc[A_V:A_V + 1]
    ot = jnp.concatenate([ot, jnp.zeros((LANES - A_V, tq), F32)], axis=0)
    o_ref[0] = ot.T.astype(o_ref.dtype)


def _mla_attn(q, k, v, tq, sub, steps):
    Bsz, S, width = q.shape
    assert S % tq == 0 and S % (sub * steps) == 0 and steps % 2 == 0
    return pl.pallas_call(
        functools.partial(_mla_attn_kernel, sub=sub, steps=steps),
        out_shape=jax.ShapeDtypeStruct((Bsz, S, width), BF16),
        grid=(Bsz, width // LANES, S // tq),
        in_specs=[pl.BlockSpec((1, tq, LANES), lambda b, h, i: (b, i, h)),
                  pl.BlockSpec((1, S, LANES), lambda b, h, i: (b, 0, h)),
                  pl.BlockSpec((1, S, LANES), lambda b, h, i: (b, 0, h))],
        out_specs=pl.BlockSpec((1, tq, LANES), lambda b, h, i: (b, i, h)),
        scratch_shapes=[pltpu.VMEM((MLA_VROWS, S), BF16),
                        pltpu.VMEM((2, sub, tq), F32), pltpu.VMEM((2, sub, tq), BF16)],
        compiler_params=_params("parallel", "parallel", "arbitrary"),
        name="mla_attn",
    )(q, k, v)


def _chunk_cumsum(x, chunk, reverse):
    n, w = x.shape
    sub = 8
    per = chunk // sub
    x3 = x.reshape(n // sub, sub, w)
    row = lax.broadcasted_iota(jnp.int32, x3.shape, 1)
    d = 1
    while d < sub:
        if reverse:
            x3 = x3 + jnp.where(row < sub - d, pltpu.roll(x3, sub - d, 1), 0.0)
        else:
            x3 = x3 + jnp.where(row >= d, pltpu.roll(x3, d, 1), 0.0)
        d *= 2
    xc = x3.reshape(n // chunk, chunk, w)
    parts = [xc[:, i * sub:(i + 1) * sub, :] for i in range(per)]
    edge = 0 if reverse else sub - 1
    carry = None
    for i in (range(per - 1, -1, -1) if reverse else range(per)):
        total = parts[i][:, edge:edge + 1, :]
        if carry is not None:
            parts[i] = parts[i] + carry
            carry = carry + total
        else:
            carry = total
    return jnp.concatenate(parts, axis=1).reshape(n, w)


def _hgrn_kernel(*refs, reverse, finalize):
    if finalize:
        (q_ref, z_ref, i_ref, lb_ref, prev_ref, og_ref, gout_ref, o_ref, st_ref, sall_ref) = refs
    else:
        (q_ref, z_ref, i_ref, lb_ref, o_ref, st_ref, sall_ref) = refs
    C = HGRN_CHUNK
    tb = q_ref.shape[0]
    nc = tb // C
    half = C // 2

    @pl.when(pl.program_id(2) == 0)
    def _():
        st_ref[...] = jnp.zeros_like(st_ref)

    q = q_ref[...]
    z = z_ref[...]
    lb = lb_ref[...]
    v = i_ref[...]
    sg = jax.nn.sigmoid(z)
    f = lb + (1.0 - lb) * sg
    key = (1.0 - lb) * (1.0 - sg)
    b = _chunk_cumsum(jnp.log(jnp.maximum(f, TINY)), C, reverse)

    b3 = b.reshape(nc, C, 2 * B_DK)
    if reverse:
        ref3 = b3[:, half:half + 1, :]
        tot3 = b3[:, 0:1, :]
    else:
        ref3 = b3[:, half - 1:half, :]
        tot3 = b3[:, C - 1:C, :]
    qs = q.reshape(nc, C, 2 * B_DK) * jnp.exp(b3 - ref3)
    ks = key.reshape(nc, C, 2 * B_DK) * jnp.exp(ref3 - b3)
    q_dec = (qs * jnp.exp(ref3)).astype(BF16)
    k_end = (ks * jnp.exp(tot3 - ref3)).astype(BF16)
    dec = jnp.exp(tot3)
    qs = qs.astype(BF16)
    ks = ks.astype(BF16)
    v3 = v.astype(BF16).reshape(nc, C, 2 * B_DV)

    t_idx = lax.broadcasted_iota(jnp.int32, (nc, C, C), 1)
    s_idx = lax.broadcasted_iota(jnp.int32, (nc, C, C), 2)
    causal = (s_idx >= t_idx) if reverse else (s_idx <= t_idx)

    outs = []
    for h in range(2):
        sl = slice(h * B_DK, (h + 1) * B_DK)
        scores = jnp.einsum('ctk,csk->cts', qs[:, :, sl], ks[:, :, sl], preferred_element_type=F32)
        scores = jnp.where(causal, scores, 0.0).astype(BF16)
        upd = jnp.einsum('csv,csk->cvk', v3, k_end[:, :, sl], preferred_element_type=F32)
        order = range(nc - 1, -1, -1) if reverse else range(nc)
        state = st_ref[h]
        for c in order:
            sall_ref[h, c] = state.astype(BF16)
            state = state * dec[c, :, sl] + upd[c]
        st_ref[h] = state
        o_h = (jnp.einsum('cts,csv->ctv', scores, v3, preferred_element_type=F32)
               + jnp.einsum('ctk,cvk->ctv', q_dec[:, :, sl], sall_ref[h], preferred_element_type=F32))
        outs.append(o_h.reshape(tb, 2 * B_DV))
    lane = lax.broadcasted_iota(jnp.int32, (tb, 2 * B_DV), 1)
    first = lane < B_DV
    o = jnp.where(first, outs[0], outs[1])
    if not finalize:
        o_ref[...] = o
        return
    o = o + prev_ref[...]
    sq = o * o
    ms0 = jnp.sum(jnp.where(first, sq, 0.0), axis=-1, keepdims=True)
    ms1 = jnp.sum(jnp.where(first, 0.0, sq), axis=-1, keepdims=True)
    ms = jnp.where(first, ms0, ms1) * (1.0 / B_DV)
    y = o * lax.rsqrt(ms + EPS) * gout_ref[...]
    o_ref[...] = (y * jax.nn.silu(og_ref[...])).astype(o_ref.dtype)


def _hgrn_direction(zb, lb, prev, gout, seq, reverse):
    T = zb.shape[0]
    Bsz = T // seq
    tb = HGRN_BLOCK
    nblk = seq // tb
    pairs = B_HEADS // 2
    dk2, dv2 = 2 * B_DK, 2 * B_DV
    finalize = prev is not None

    def row(b, p, i):
        return b * nblk + (nblk - 1 - i if reverse else i)

    z_base = (2 if reverse else 1) * (B_HEADS * B_DK) // dk2
    i_base = 3 * (B_HEADS * B_DK) // dv2
    g_base = i_base + (B_HEADS * B_DV) // dv2
    in_specs = [pl.BlockSpec((tb, dk2), lambda b, p, i: (row(b, p, i), p)),
                pl.BlockSpec((tb, dk2), lambda b, p, i: (row(b, p, i), z_base + p)),
                pl.BlockSpec((tb, dv2), lambda b, p, i: (row(b, p, i), i_base + p)),
                pl.BlockSpec((1, dk2), lambda b, p, i: (0, p))]
    args = [zb, zb, zb, lb.reshape(1, -1)]
    if finalize:
        in_specs += [pl.BlockSpec((tb, dv2), lambda b, p, i: (row(b, p, i), p)),
                     pl.BlockSpec((tb, dv2), lambda b, p, i: (row(b, p, i), g_base + p)),
                     pl.BlockSpec((1, dv2), lambda b, p, i: (0, 0))]
        args += [prev, zb, jnp.tile(gout, 2).reshape(1, dv2)]
    return pl.pallas_call(
        functools.partial(_hgrn_kernel, reverse=reverse, finalize=finalize),
        out_shape=jax.ShapeDtypeStruct((T, B_HEADS * B_DV), BF16 if finalize else F32),
        grid=(Bsz, pairs, nblk),
        in_specs=in_specs,
        out_specs=pl.BlockSpec((tb, dv2), lambda b, p, i: (row(b, p, i), p)),
        scratch_shapes=[pltpu.VMEM((2, dv2, B_DK), F32),
                        pltpu.VMEM((2, tb // HGRN_CHUNK, dv2, B_DK), BF16)],
        compiler_params=_params("parallel", "parallel", "arbitrary"),
        name="hgrn_bwd" if reverse else "hgrn_fwd",
    )(*args)


def _t5_bucket(rel):
    nb = REL_BUCKETS // 2
    max_exact = nb // 2
    ret = (rel > 0).astype(jnp.int32) * nb
    n = jnp.abs(rel)
    large = max_exact + (jnp.log(jnp.maximum(n, 1).astype(F32) / max_exact)
                         / math.log(REL_MAX_DIST / max_exact) * (nb - max_exact)).astype(jnp.int32)
    large = jnp.minimum(large, nb - 1)
    return ret + jnp.where(n < max_exact, n, large)


def _win_kernel(sink_ref, q_ref, kp_ref, kc_ref, kn_ref, vp_ref, vc_ref, vn_ref, bias_ref, mask_ref, o_ref):
    G = C_HEADS // C_KV_HEADS
    blk = C_BLOCK
    n = pl.program_id(1)
    last = pl.num_programs(1) - 1
    rows, span = G * blk, 3 * blk

    which = jnp.where(n == 0, 1, jnp.where(n == last, 2, 0))
    valid = mask_ref[which] != 0.0
    g_idx = lax.broadcasted_iota(jnp.int32, (rows, 1), 0) // blk

    def scores(kvh):
        grp = slice(kvh * LANES, (kvh + 1) * LANES)
        q = jnp.concatenate([q_ref[:, (kvh * G + g) * LANES:(kvh * G + g + 1) * LANES] for g in range(G)],
                            axis=0)
        k = jnp.concatenate([kp_ref[:, grp], kc_ref[:, grp], kn_ref[:, grp]], axis=0)
        s = lax.dot_general(q, k, NT_DIMS, preferred_element_type=F32) * (C_DH ** -0.5)
        s = s + bias_ref[kvh].reshape(rows, span)
        return jnp.where(valid, s, MASK_VALUE)

    all_scores = [scores(kvh) for kvh in range(C_KV_HEADS)]
    for kvh, s in enumerate(all_scores):
        grp = slice(kvh * LANES, (kvh + 1) * LANES)
        v = jnp.concatenate([vp_ref[:, grp], vc_ref[:, grp], vn_ref[:, grp]], axis=0)
        sink = jnp.zeros((rows, 1), F32)
        for g in range(G):
            sink = jnp.where(g_idx == g, sink_ref[kvh * G + g], sink)
        m = jnp.maximum(jnp.max(s, axis=-1, keepdims=True), sink)
        p = jnp.exp(s - m)
        p = p / (jnp.sum(p, axis=-1, keepdims=True) + jnp.exp(sink - m))
        o = jnp.dot(p.astype(BF16), v, preferred_element_type=F32)
        for pair in range(G // 2):
            first = o[(2 * pair) * blk:(2 * pair + 1) * blk]
            second = pltpu.roll(o[(2 * pair + 1) * blk:(2 * pair + 2) * blk], C_DH, 1)
            col = (kvh * G // 2 + pair) * LANES
            o_ref[:, col:col + LANES] = (first + second).astype(o_ref.dtype)


def _window_gqa(zc, bias, mask, sink, seq):
    T = zc.shape[0]
    blk = C_BLOCK
    nb = seq // blk
    assert nb >= 2
    qw = C_HEADS * LANES
    kw = C_KV_HEADS * LANES
    G = C_HEADS // C_KV_HEADS

    def kv_spec(col, shift):
        return pl.BlockSpec((blk, kw), lambda b, n, sref: (b * nb + jnp.clip(n + shift, 0, nb - 1), col))

    k_col, v_col = qw // kw, qw // kw + 1
    grid_spec = pltpu.PrefetchScalarGridSpec(
        num_scalar_prefetch=1,
        grid=(T // seq, nb),
        in_specs=[pl.BlockSpec((blk, qw), lambda b, n, sref: (b * nb + n, 0)),
                  kv_spec(k_col, -1), kv_spec(k_col, 0), kv_spec(k_col, 1),
                  kv_spec(v_col, -1), kv_spec(v_col, 0), kv_spec(v_col, 1),
                  pl.BlockSpec((C_KV_HEADS, G, blk, 3 * blk), lambda b, n, sref: (0, 0, 0, 0)),
                  pl.BlockSpec(mask.shape, lambda b, n, sref: (0, 0, 0))],
        out_specs=pl.BlockSpec((blk, C_HEADS * C_DH), lambda b, n, sref: (b * nb + n, 0)))
    return pl.pallas_call(
        _win_kernel,
        out_shape=jax.ShapeDtypeStruct((T, C_HEADS * C_DH), BF16),
        grid_spec=grid_spec,
        compiler_params=_params("parallel", "arbitrary"),
        name="window_gqa",
    )(sink, zc, zc, zc, zc, zc, zc, zc, bias, mask)


def _merge_kernel(x_ref, ya_ref, yb_ref, yc_ref, ga_ref, gb_ref, gc_ref,
                  wa_ref, wb_ref, wc_ref, wo_ref, o_ref):
    def branch(y_ref, gate_ref, w_ref):
        return jax.nn.sigmoid(gate_ref[...]) * jnp.dot(y_ref[...], w_ref[...], preferred_element_type=F32)

    merged = (branch(ya_ref, ga_ref, wa_ref) + branch(yb_ref, gb_ref, wb_ref)
              + branch(yc_ref, gc_ref, wc_ref))
    o_ref[...] = x_ref[...] + jnp.dot(merged.astype(BF16), wo_ref[...], preferred_element_type=F32)


def _merge(x, ya, yb, yc, gates, wa, wb, wc, wo, tm):
    T, D = x.shape
    row = lambda i: (i, 0)
    return pl.pallas_call(
        _merge_kernel,
        out_shape=jax.ShapeDtypeStruct((T, D), F32),
        grid=(T // tm,),
        in_specs=[pl.BlockSpec((tm, D), row),
                  pl.BlockSpec((tm, ya.shape[1]), row),
                  pl.BlockSpec((tm, yb.shape[1]), row),
                  pl.BlockSpec((tm, yc.shape[1]), row),
                  pl.BlockSpec((tm, D), lambda i: (i, 0)),
                  pl.BlockSpec((tm, D), lambda i: (i, 1)),
                  pl.BlockSpec((tm, D), lambda i: (i, 2)),
                  _resident(wa.shape), _resident(wb.shape), _resident(wc.shape), _resident(wo.shape)],
        out_specs=pl.BlockSpec((tm, D), row),
        compiler_params=_params("parallel"),
        name="merge",
    )(x, ya, yb, yc, gates, gates, gates, wa, wb, wc, wo)


def _cross_kernel(x_ref, g_ref, kv_ref, wq_ref, wo_ref, o_ref):
    x = x_ref[...]
    h = _rms(x, g_ref[...]).astype(BF16)
    q = jnp.dot(h, wq_ref[...], preferred_element_type=F32).astype(BF16)
    kv = kv_ref[0]
    width = X_HEADS * X_DH
    outs = []
    for hd in range(X_HEADS):
        sl = slice(hd * X_DH, (hd + 1) * X_DH)
        k = kv[:, sl]
        v = kv[:, width + hd * X_DH: width + (hd + 1) * X_DH]
        s = lax.dot_general(q[:, sl], k, NT_DIMS, preferred_element_type=F32) * (X_DH ** -0.5)
        p = jnp.exp(s - jnp.max(s, axis=-1, keepdims=True))
        p = p / jnp.sum(p, axis=-1, keepdims=True)
        outs.append(jnp.dot(p.astype(BF16), v, preferred_element_type=F32).astype(BF16))
    o = jnp.concatenate(outs, axis=-1)
    o_ref[...] = x + jnp.dot(o, wo_ref[...], preferred_element_type=F32)


def _cross(x, g, kv, wq, wo, seq, tm):
    T, D = x.shape
    nseq = seq // tm
    row = lambda i: (i, 0)
    return pl.pallas_call(
        _cross_kernel,
        out_shape=jax.ShapeDtypeStruct((T, D), F32),
        grid=(T // tm,),
        in_specs=[pl.BlockSpec((tm, D), row),
                  _resident((1, D)),
                  pl.BlockSpec((1,) + kv.shape[1:], lambda i: (i // nseq, 0, 0)),
                  _resident(wq.shape), _resident(wo.shape)],
        out_specs=pl.BlockSpec((tm, D), row),
        compiler_params=_params("parallel"),
        name="cross_attn",
    )(x, g.reshape(1, D), kv, wq, wo)


def _swiglu_kernel(x_ref, g_ref, w1_ref, w3_ref, w2_ref, gf_ref, o_ref, *, final_norm):
    x = x_ref[...]
    h = _rms(x, g_ref[...]).astype(BF16)
    a = jnp.dot(h, w1_ref[...], preferred_element_type=F32)
    b = jnp.dot(h, w3_ref[...], preferred_element_type=F32)
    u = (jax.nn.silu(a) * b).astype(BF16)
    y = x + jnp.dot(u, w2_ref[...], preferred_element_type=F32)
    if final_norm:
        y = _rms(y, gf_ref[...])
    o_ref[...] = y


def _swiglu(x, g, w1, w3, w2, g_final, final_norm, tm):
    T, D = x.shape
    row = lambda i: (i, 0)
    return pl.pallas_call(
        functools.partial(_swiglu_kernel, final_norm=final_norm),
        out_shape=jax.ShapeDtypeStruct((T, D), F32),
        grid=(T // tm,),
        in_specs=[pl.BlockSpec((tm, D), row), _resident((1, D)),
                  _resident(w1.shape), _resident(w3.shape), _resident(w2.shape), _resident((1, D))],
        out_specs=pl.BlockSpec((tm, D), row),
        compiler_params=_params("parallel"),
        name="swiglu",
    )(x, g.reshape(1, D), w1, w3, w2, g_final.reshape(1, D))


def _pad_cols(w, width):
    return jnp.pad(w, ((0, 0), (0, width - w.shape[1])))


def _group_cols(w, heads, width):
    K = w.shape[0]
    w = w.reshape(K, heads, -1)
    return jnp.pad(w, ((0, 0), (0, 0), (0, width - w.shape[2]))).reshape(K, heads * width)


def _group_rows(w, heads, width):
    N = w.shape[1]
    w = w.reshape(heads, -1, N)
    return jnp.pad(w, ((0, 0), (0, width - w.shape[1]), (0, 0))).reshape(heads * width, N)


def _split_in_proj(w):
    o = 0
    parts = []
    for n in (A_Q_RANK, A_KV_RANK, A_ROPE, 3 * B_HEADS * B_DK + 2 * B_HEADS * B_DV,
              C_HEADS * C_DH, C_KV_HEADS * C_DH, C_KV_HEADS * C_DH, 3 * w.shape[0]):
        parts.append(w[:, o:o + n])
        o += n
    cq, ckv, kr, wb, wcq, wck, wcv, wg = parts
    half = A_ROPE // 2
    wa = jnp.concatenate([cq, ckv, _pad_cols(kr[:, :half], LANES), _pad_cols(kr[:, half:], LANES)], axis=1)
    wc = jnp.concatenate([_group_cols(wcq, C_HEADS, LANES), _group_cols(wck, C_KV_HEADS, LANES),
                          _group_cols(wcv, C_KV_HEADS, LANES)], axis=1)
    return [t.astype(BF16) for t in (wa, wb, wc, wg)]


def _mla_weights(wuq, wukv):
    half = A_ROPE // 2
    wq = wuq.reshape(A_Q_RANK, A_HEADS, A_NOPE + A_ROPE)
    wqn = _group_cols(wq[:, :, :A_NOPE].reshape(A_Q_RANK, -1), A_HEADS, LANES)
    wqr = jnp.concatenate([wq[:, :, A_NOPE:A_NOPE + half].reshape(A_Q_RANK, -1),
                           wq[:, :, A_NOPE + half:].reshape(A_Q_RANK, -1)], axis=1)
    wkv = wukv.reshape(A_KV_RANK, A_HEADS, A_NOPE + A_V)
    wkn = _group_cols(wkv[:, :, :A_NOPE].reshape(A_KV_RANK, -1), A_HEADS, LANES)
    wv = _group_cols(wkv[:, :, A_NOPE:].reshape(A_KV_RANK, -1), A_HEADS, LANES)
    return [t.astype(BF16) for t in (wqn, wqr, wkn, wv)]


def _mla_placement():
    half = A_ROPE // 2
    src = jnp.arange(LANES)[:, None]
    dst = jnp.arange(A_HEADS * LANES)[None, :]
    head, lane = dst // LANES, dst % LANES
    mats = []
    for per_head, base in ((True, A_NOPE), (True, A_NOPE + half), (False, A_NOPE), (False, A_NOPE + half)):
        j = lane - base
        want = head * half + j if per_head else j
        mats.append(((j >= 0) & (j < half) & (src == want)).astype(BF16))
    ones_row = (lane == A_V).astype(F32)
    return jnp.stack(mats), ones_row


def kernel(x, mem, w_in, g_mix, a_gq, a_gkv, a_wuq, a_wukv, b_lb, b_gout, c_sink, rel_bias,
           w_br_a, w_br_b, w_br_c, w_out, g_x, g_mem, x_wq, x_wkv, x_wo, g_ffn,
           f_w1, f_w3, f_w2, g_final):
    Bsz, S, D = x.shape
    depth = w_in.shape[0]
    T = Bsz * S
    M = mem.shape[1]
    tm = ROW_TILE
    half = A_ROPE // 2

    inv = ROPE_THETA ** (-jnp.arange(half, dtype=F32) / half)
    ang = jnp.arange(S, dtype=jnp.int32).astype(F32)[:, None] * inv[None, :]
    cos = jnp.tile(jnp.cos(ang), (1, LANES // half))
    sin = jnp.tile(jnp.sin(ang), (1, LANES // half))
    place, ones_row = _mla_placement()

    sm = jax.nn.softmax(b_lb.astype(F32), axis=1)
    lower = jnp.cumsum(sm, axis=1) - sm[:, :1]

    span = 3 * C_BLOCK
    rel = jnp.arange(span)[None, :] - C_BLOCK - jnp.arange(C_BLOCK)[:, None]
    G = C_HEADS // C_KV_HEADS
    onehot = (_t5_bucket(rel)[None] == jnp.arange(REL_BUCKETS)[:, None, None]).astype(F32)
    bias = jnp.einsum('nh,nqs->hqs', rel_bias.astype(F32), onehot,
                      precision=lax.Precision.HIGHEST).reshape(C_KV_HEADS, G, C_BLOCK, span)
    col = jnp.arange(span)[None, :]
    band = jnp.abs(rel) <= C_WINDOW
    win_mask = jnp.stack([band, band & (col >= C_BLOCK), band & (col < 2 * C_BLOCK)]).astype(F32)
    win_mask = jnp.tile(win_mask, (1, G, 1))

    xt = x.reshape(T, D)
    mem2 = mem.reshape(Bsz * M, D)
    for l in range(depth):
        za, zb, zc, zg = _rms_proj(xt, g_mix[l], _split_in_proj(w_in[l]), (F32, F32, BF16, F32), IN_PROJ_TILE)

        qa, ka, va = _mla_prep(za, a_gq[l], a_gkv[l], *_mla_weights(a_wuq[l], a_wukv[l]),
                               place, ones_row, cos, sin, S, tm)
        shp = (Bsz, S, A_HEADS * LANES)
        ya = _mla_attn(qa.reshape(shp), ka.reshape(shp), va.reshape(shp),
                       MLA_TQ, MLA_SUB, MLA_STEPS).reshape(T, -1)

        of = _hgrn_direction(zb, lower[0, l], None, None, S, reverse=False)
        yb = _hgrn_direction(zb, lower[1, l], of, b_gout[l], S, reverse=True)

        yc = _window_gqa(zc, bias, win_mask, c_sink[l].astype(F32), S)

        xt = _merge(xt, ya, yb, yc, zg,
                    _group_rows(w_br_a[l], A_HEADS, LANES).astype(BF16), w_br_b[l].astype(BF16),
                    w_br_c[l].astype(BF16), w_out[l].astype(BF16), tm)

        (kvm,) = _rms_proj(mem2, g_mem[l], [x_wkv[l].astype(BF16)], (BF16,), Bsz * M)
        xt = _cross(xt, g_x[l], kvm.reshape(Bsz, M, -1), x_wq[l].astype(BF16), x_wo[l].astype(BF16), S, tm)

        xt = _swiglu(xt, g_ffn[l], f_w1[l].astype(BF16), f_w3[l].astype(BF16), f_w2[l].astype(BF16),
                     g_final, l == depth - 1, tm)
    return xt.reshape(Bsz, S, D)
```

```python
import functools
import math

import jax
import jax.numpy as jnp
from jax import lax
from jax.experimental import pallas as pl
from jax.experimental.pallas import tpu as pltpu

F32 = jnp.float32
BF16 = jnp.bfloat16

EPS = 1e-6
MASK_VALUE = -1e30
TINY = 1e-30
ROPE_THETA = 10000.0

A_HEADS, A_NOPE, A_ROPE, A_V = 8, 64, 32, 64
A_Q_RANK, A_KV_RANK = 384, 256
B_HEADS, B_DK, B_DV = 8, 128, 64
C_HEADS, C_KV_HEADS, C_DH, C_WINDOW, C_BLOCK = 8, 2, 64, 128, 128
REL_BUCKETS, REL_MAX_DIST = 32, 128
X_HEADS, X_DH = 4, 256

LANES = 128
VMEM_LIMIT = 56 * 1024 * 1024

ROW_TILE = 512
IN_PROJ_TILE = 256
MLA_TQ = 1024
MLA_SUB = 256
MLA_STEPS = 32
MLA_SLAB = 16
MLA_VROWS = 80
HGRN_CHUNK = 32
HGRN_BLOCK = 1024

NT_DIMS = (((1,), (1,)), ((), ()))


def _params(*sem):
    return pltpu.CompilerParams(dimension_semantics=sem, vmem_limit_bytes=VMEM_LIMIT)


def _rms(x, g):
    return x * lax.rsqrt(jnp.mean(x * x, axis=-1, keepdims=True) + EPS) * g


def _resident(shape):
    return pl.BlockSpec(shape, lambda *_: (0,) * len(shape), pipeline_mode=pl.Buffered(1))


def _rms_proj_kernel(*refs, n_out):
    x_ref, g_ref = refs[:2]
    w_refs = refs[2:2 + n_out]
    o_refs = refs[2 + n_out:]
    h = _rms(x_ref[...], g_ref[...]).astype(BF16)
    for w_ref, o_ref in zip(w_refs, o_refs):
        o_ref[...] = jnp.dot(h, w_ref[...], preferred_element_type=F32).astype(o_ref.dtype)


def _rms_proj(x, g, weights, out_dtypes, tm):
    T, D = x.shape
    row = lambda i: (i, 0)
    return pl.pallas_call(
        functools.partial(_rms_proj_kernel, n_out=len(weights)),
        out_shape=tuple(jax.ShapeDtypeStruct((T, w.shape[1]), dt) for w, dt in zip(weights, out_dtypes)),
        grid=(T // tm,),
        in_specs=[pl.BlockSpec((tm, D), row), _resident((1, D))] + [_resident(w.shape) for w in weights],
        out_specs=tuple(pl.BlockSpec((tm, w.shape[1]), row) for w in weights),
        compiler_params=_params("parallel"),
        name="rms_proj",
    )(x, g.reshape(1, D), *weights)


def _mla_prep_kernel(za_ref, gq_ref, gkv_ref, wqn_ref, wqr_ref, wkn_ref, wv_ref, place_ref, ones_ref,
                     cos_ref, sin_ref, q_ref, k_ref, v_ref, *, q_scale):
    za = za_ref[...]
    cq = za[:, :A_Q_RANK]
    ckv = za[:, A_Q_RANK:A_Q_RANK + A_KV_RANK]
    kr1 = za[:, 640:768]
    kr2 = za[:, 768:896]
    cos = cos_ref[...]
    sin = sin_ref[...]

    def place(x, idx):
        return jnp.dot(x.astype(BF16), place_ref[idx], preferred_element_type=F32)

    hq = _rms(cq, gq_ref[...]).astype(BF16)
    qr = jnp.dot(hq, wqr_ref[...], preferred_element_type=F32)
    q1 = qr[:, :LANES]
    q2 = qr[:, LANES:]
    qn = jnp.dot(hq, wqn_ref[...], preferred_element_type=F32)
    q_ref[...] = (qn * q_scale + place((q1 * cos - q2 * sin) * q_scale, 0)
                  + place((q1 * sin + q2 * cos) * q_scale, 1)).astype(BF16)

    hkv = _rms(ckv, gkv_ref[...]).astype(BF16)
    kn = jnp.dot(hkv, wkn_ref[...], preferred_element_type=F32)
    k_ref[...] = (kn + place(kr1 * cos - kr2 * sin, 2) + place(kr1 * sin + kr2 * cos, 3)).astype(BF16)
    v_ref[...] = (jnp.dot(hkv, wv_ref[...], preferred_element_type=F32) + ones_ref[...]).astype(BF16)


def _mla_prep(za, gq, gkv, wqn, wqr, wkn, wv, place, ones_row, cos, sin, seq, tm):
    T = za.shape[0]
    width = A_HEADS * LANES
    nseq = seq // tm
    q_scale = (A_NOPE + A_ROPE) ** -0.5 * math.log2(math.e)
    row = lambda i: (i, 0)
    pos = lambda i: (i % nseq, 0)
    out = jax.ShapeDtypeStruct((T, width), BF16)
    return pl.pallas_call(
        functools.partial(_mla_prep_kernel, q_scale=q_scale),
        out_shape=(out, out, out),
        grid=(T // tm,),
        in_specs=[pl.BlockSpec((tm, za.shape[1]), row),
                  _resident((1, A_Q_RANK)), _resident((1, A_KV_RANK)),
                  _resident(wqn.shape), _resident(wqr.shape), _resident(wkn.shape), _resident(wv.shape),
                  _resident(place.shape), _resident(ones_row.shape),
                  pl.BlockSpec((tm, LANES), pos), pl.BlockSpec((tm, LANES), pos)],
        out_specs=(pl.BlockSpec((tm, width), row),) * 3,
        compiler_params=_params("parallel"),
        name="mla_prep",
    )(za, gq.reshape(1, -1), gkv.reshape(1, -1), wqn, wqr, wkn, wv, place, ones_row, cos, sin)


def _mla_attn_kernel(q_ref, k_ref, v_ref, o_ref, vt_ref, s_buf, p_buf, *, sub, steps):
    tq = q_ref.shape[1]
    seq = k_ref.shape[1]
    n_sub = seq // sub

    @pl.when(pl.program_id(2) == 0)
    def _():
        def tr(c, carry):
            off = pl.multiple_of(c * sub, sub)
            vt = v_ref[0, pl.ds(off, sub), :].astype(F32).T
            vt_ref[:, pl.ds(off, sub)] = vt[:MLA_VROWS].astype(BF16)
            return carry
        lax.fori_loop(0, n_sub, tr, 0)

    qt = q_ref[0].astype(F32).T.astype(BF16)

    def qk(j, slot):
        off = pl.multiple_of(jnp.minimum(j, n_sub - 1) * sub, sub)
        s = jnp.dot(k_ref[0, pl.ds(off, sub), :], qt, preferred_element_type=F32)
        s_buf[slot] = s
        return jnp.max(s, axis=0, keepdims=True)

    def step(j, slot, m, acc, cmax):
        cmax_next = qk(j + 1, 1 - slot)
        m_new = jnp.maximum(m, cmax)
        alpha = jnp.exp2(m - m_new)
        mb = jnp.broadcast_to(m_new, (MLA_SLAB, tq))
        for r in range(sub // MLA_SLAB):
            rows = pl.ds(r * MLA_SLAB, MLA_SLAB)
            p_buf[slot, rows, :] = jnp.exp2(s_buf[slot, rows, :] - mb).astype(BF16)
        off = pl.multiple_of(j * sub, sub)
        acc = alpha * acc + jnp.dot(vt_ref[:, pl.ds(off, sub)], p_buf[slot], preferred_element_type=F32)
        return m_new, acc, cmax_next

    def body(i, carry):
        for u in range(steps):
            carry = step(steps * i + u, u % 2, *carry)
        return carry

    m0 = jnp.full((1, tq), MASK_VALUE, F32)
    acc0 = jnp.zeros((MLA_VROWS, tq), F32)
    _, acc, _ = lax.fori_loop(0, n_sub // steps, body, (m0, acc0, qk(0, 0)))
    ot = acc[:A_V] / acc[A_V:A_V + 1]
    ot = jnp.concatenate([ot, jnp.zeros((LANES - A_V, tq), F32)], axis=0)
    o_ref[0] = ot.T.astype(o_ref.dtype)


def _mla_attn(q, k, v, tq, sub, steps):
    Bsz, S, width = q.shape
    assert S % tq == 0 and S % (sub * steps) == 0 and steps % 2 == 0
    return pl.pallas_call(
        functools.partial(_mla_attn_kernel, sub=sub, steps=steps),
        out_shape=jax.ShapeDtypeStruct((Bsz, S, width), BF16),
        grid=(Bsz, width // LANES, S // tq),
        in_specs=[pl.BlockSpec((1, tq, LANES), lambda b, h, i: (b, i, h)),
                  pl.BlockSpec((1, S, LANES), lambda b, h, i: (b, 0, h)),
                  pl.BlockSpec((1, S, LANES), lambda b, h, i: (b, 0, h))],
        out_specs=pl.BlockSpec((1, tq, LANES), lambda b, h, i: (b, i, h)),
        scratch_shapes=[pltpu.VMEM((MLA_VROWS, S), BF16),
                        pltpu.VMEM((2, sub, tq), F32), pltpu.VMEM((2, sub, tq), BF16)],
        compiler_params=_params("parallel", "parallel", "arbitrary"),
        name="mla_attn",
    )(q, k, v)


def _chunk_cumsum(x, chunk, reverse):
    n, w = x.shape
    sub = 8
    per = chunk // sub
    x3 = x.reshape(n // sub, sub, w)
    row = lax.broadcasted_iota(jnp.int32, x3.shape, 1)
    d = 1
    while d < sub:
        if reverse:
            x3 = x3 + jnp.where(row < sub - d, pltpu.roll(x3, sub - d, 1), 0.0)
        else:
            x3 = x3 + jnp.where(row >= d, pltpu.roll(x3, d, 1), 0.0)
        d *= 2
    xc = x3.reshape(n // chunk, chunk, w)
    parts = [xc[:, i * sub:(i + 1) * sub, :] for i in range(per)]
    edge = 0 if reverse else sub - 1
    carry = None
    for i in (range(per - 1, -1, -1) if reverse else range(per)):
        total = parts[i][:, edge:edge + 1, :]
        if carry is not None:
            parts[i] = parts[i] + carry
            carry = carry + total
        else:
            carry = total
    return jnp.concatenate(parts, axis=1).reshape(n, w)


def _hgrn_kernel(*refs, reverse, finalize):
    if finalize:
        (q_ref, z_ref, i_ref, lb_ref, prev_ref, og_ref, gout_ref, o_ref, st_ref, sall_ref) = refs
    else:
        (q_ref, z_ref, i_ref, lb_ref, o_ref, st_ref, sall_ref) = refs
    C = HGRN_CHUNK
    tb = q_ref.shape[0]
    nc = tb // C
    half = C // 2

    @pl.when(pl.program_id(2) == 0)
    def _():
        st_ref[...] = jnp.zeros_like(st_ref)

    q = q_ref[...]
    z = z_ref[...]
    lb = lb_ref[...]
    v = i_ref[...]
    sg = jax.nn.sigmoid(z)
    f = lb + (1.0 - lb) * sg
    key = (1.0 - lb) * (1.0 - sg)
    b = _chunk_cumsum(jnp.log(jnp.maximum(f, TINY)), C, reverse)

    b3 = b.reshape(nc, C, 2 * B_DK)
    if reverse:
        ref3 = b3[:, half:half + 1, :]
        tot3 = b3[:, 0:1, :]
    else:
        ref3 = b3[:, half - 1:half, :]
        tot3 = b3[:, C - 1:C, :]
    qs = q.reshape(nc, C, 2 * B_DK) * jnp.exp(b3 - ref3)
    ks = key.reshape(nc, C, 2 * B_DK) * jnp.exp(ref3 - b3)
    q_dec = (qs * jnp.exp(ref3)).astype(BF16)
    k_end = (ks * jnp.exp(tot3 - ref3)).astype(BF16)
    dec = jnp.exp(tot3)
    qs = qs.astype(BF16)
    ks = ks.astype(BF16)
    v3 = v.astype(BF16).reshape(nc, C, 2 * B_DV)

    t_idx = lax.broadcasted_iota(jnp.int32, (nc, C, C), 1)
    s_idx = lax.broadcasted_iota(jnp.int32, (nc, C, C), 2)
    causal = (s_idx >= t_idx) if reverse else (s_idx <= t_idx)

    lane_v = lax.broadcasted_iota(jnp.int32, v3.shape, 2)
    v_own = [jnp.where(lane_v < B_DV, v3, jnp.zeros_like(v3)), jnp.where(lane_v >= B_DV, v3, jnp.zeros_like(v3))]
    upd = sum(jnp.einsum('csv,csk->cvk', v_own[h], k_end[:, :, h * B_DK:(h + 1) * B_DK],
                         preferred_element_type=F32) for h in range(2))
    row_v = lax.broadcasted_iota(jnp.int32, (2 * B_DV, B_DK), 0)
    state = st_ref[...]
    for c in (range(nc - 1, -1, -1) if reverse else range(nc)):
        sall_ref[c] = state.astype(BF16)
        decay = jnp.where(row_v < B_DV, dec[c, :, :B_DK], dec[c, :, B_DK:])
        state = state * decay + upd[c]
    st_ref[...] = state

    outs = []
    for h in range(2):
        sl = slice(h * B_DK, (h + 1) * B_DK)
        scores = jnp.einsum('ctk,csk->cts', qs[:, :, sl], ks[:, :, sl], preferred_element_type=F32)
        scores = jnp.where(causal, scores, 0.0).astype(BF16)
        o_h = (jnp.einsum('cts,csv->ctv', scores, v3, preferred_element_type=F32)
               + jnp.einsum('ctk,cvk->ctv', q_dec[:, :, sl], sall_ref[...], preferred_element_type=F32))
        outs.append(o_h.reshape(tb, 2 * B_DV))
    lane = lax.broadcasted_iota(jnp.int32, (tb, 2 * B_DV), 1)
    first = lane < B_DV
    o = jnp.where(first, outs[0], outs[1])
    if not finalize:
        o_ref[...] = o
        return
    o = o + prev_ref[...]
    sq = o * o
    ms0 = jnp.sum(jnp.where(first, sq, 0.0), axis=-1, keepdims=True)
    ms1 = jnp.sum(jnp.where(first, 0.0, sq), axis=-1, keepdims=True)
    ms = jnp.where(first, ms0, ms1) * (1.0 / B_DV)
    y = o * lax.rsqrt(ms + EPS) * gout_ref[...]
    o_ref[...] = (y * jax.nn.silu(og_ref[...])).astype(o_ref.dtype)


def _hgrn_direction(zb, lb, prev, gout, seq, reverse):
    T = zb.shape[0]
    Bsz = T // seq
    tb = HGRN_BLOCK
    nblk = seq // tb
    pairs = B_HEADS // 2
    dk2, dv2 = 2 * B_DK, 2 * B_DV
    finalize = prev is not None

    def row(b, p, i):
        return b * nblk + (nblk - 1 - i if reverse else i)

    z_base = (2 if reverse else 1) * (B_HEADS * B_DK) // dk2
    i_base = 3 * (B_HEADS * B_DK) // dv2
    g_base = i_base + (B_HEADS * B_DV) // dv2
    in_specs = [pl.BlockSpec((tb, dk2), lambda b, p, i: (row(b, p, i), p)),
                pl.BlockSpec((tb, dk2), lambda b, p, i: (row(b, p, i), z_base + p)),
                pl.BlockSpec((tb, dv2), lambda b, p, i: (row(b, p, i), i_base + p)),
                pl.BlockSpec((1, dk2), lambda b, p, i: (0, p))]
    args = [zb, zb, zb, lb.reshape(1, -1)]
    if finalize:
        in_specs += [pl.BlockSpec((tb, dv2), lambda b, p, i: (row(b, p, i), p)),
                     pl.BlockSpec((tb, dv2), lambda b, p, i: (row(b, p, i), g_base + p)),
                     pl.BlockSpec((1, dv2), lambda b, p, i: (0, 0))]
        args += [prev, zb, jnp.tile(gout, 2).reshape(1, dv2)]
    return pl.pallas_call(
        functools.partial(_hgrn_kernel, reverse=reverse, finalize=finalize),
        out_shape=jax.ShapeDtypeStruct((T, B_HEADS * B_DV), BF16 if finalize else F32),
        grid=(Bsz, pairs, nblk),
        in_specs=in_specs,
        out_specs=pl.BlockSpec((tb, dv2), lambda b, p, i: (row(b, p, i), p)),
        scratch_shapes=[pltpu.VMEM((dv2, B_DK), F32),
                        pltpu.VMEM((tb // HGRN_CHUNK, dv2, B_DK), BF16)],
        compiler_params=_params("parallel", "parallel", "arbitrary"),
        name="hgrn_bwd" if reverse else "hgrn_fwd",
    )(*args)


def _t5_bucket(rel):
    nb = REL_BUCKETS // 2
    max_exact = nb // 2
    ret = (rel > 0).astype(jnp.int32) * nb
    n = jnp.abs(rel)
    large = max_exact + (jnp.log(jnp.maximum(n, 1).astype(F32) / max_exact)
                         / math.log(REL_MAX_DIST / max_exact) * (nb - max_exact)).astype(jnp.int32)
    large = jnp.minimum(large, nb - 1)
    return ret + jnp.where(n < max_exact, n, large)


def _win_kernel(sink_ref, q_ref, kp_ref, kc_ref, kn_ref, vp_ref, vc_ref, vn_ref, bias_ref, mask_ref, o_ref):
    G = C_HEADS // C_KV_HEADS
    blk = C_BLOCK
    n = pl.program_id(1)
    last = pl.num_programs(1) - 1
    rows, span = G * blk, 3 * blk

    which = jnp.where(n == 0, 1, jnp.where(n == last, 2, 0))
    valid = mask_ref[which] != 0.0
    g_idx = lax.broadcasted_iota(jnp.int32, (rows, 1), 0) // blk

    def scores(kvh):
        grp = slice(kvh * LANES, (kvh + 1) * LANES)
        q = jnp.concatenate([q_ref[:, (kvh * G + g) * LANES:(kvh * G + g + 1) * LANES] for g in range(G)],
                            axis=0)
        k = jnp.concatenate([kp_ref[:, grp], kc_ref[:, grp], kn_ref[:, grp]], axis=0)
        s = lax.dot_general(q, k, NT_DIMS, preferred_element_type=F32) * (C_DH ** -0.5)
        s = s + bias_ref[kvh].reshape(rows, span)
        return jnp.where(valid, s, MASK_VALUE)

    all_scores = [scores(kvh) for kvh in range(C_KV_HEADS)]
    for kvh, s in enumerate(all_scores):
        grp = slice(kvh * LANES, (kvh + 1) * LANES)
        v = jnp.concatenate([vp_ref[:, grp], vc_ref[:, grp], vn_ref[:, grp]], axis=0)
        sink = jnp.zeros((rows, 1), F32)
        for g in range(G):
            sink = jnp.where(g_idx == g, sink_ref[kvh * G + g], sink)
        m = jnp.maximum(jnp.max(s, axis=-1, keepdims=True), sink)
        p = jnp.exp(s - m)
        p = p / (jnp.sum(p, axis=-1, keepdims=True) + jnp.exp(sink - m))
        o = jnp.dot(p.astype(BF16), v, preferred_element_type=F32)
        for pair in range(G // 2):
            first = o[(2 * pair) * blk:(2 * pair + 1) * blk]
            second = pltpu.roll(o[(2 * pair + 1) * blk:(2 * pair + 2) * blk], C_DH, 1)
            col = (kvh * G // 2 + pair) * LANES
            o_ref[:, col:col + LANES] = (first + second).astype(o_ref.dtype)


def _window_gqa(zc, bias, mask, sink, seq):
    T = zc.shape[0]
    blk = C_BLOCK
    nb = seq // blk
    assert nb >= 2
    qw = C_HEADS * LANES
    kw = C_KV_HEADS * LANES
    G = C_HEADS // C_KV_HEADS

    def kv_spec(col, shift):
        return pl.BlockSpec((blk, kw), lambda b, n, sref: (b * nb + jnp.clip(n + shift, 0, nb - 1), col))

    k_col, v_col = qw // kw, qw // kw + 1
    grid_spec = pltpu.PrefetchScalarGridSpec(
        num_scalar_prefetch=1,
        grid=(T // seq, nb),
        in_specs=[pl.BlockSpec((blk, qw), lambda b, n, sref: (b * nb + n, 0)),
                  kv_spec(k_col, -1), kv_spec(k_col, 0), kv_spec(k_col, 1),
                  kv_spec(v_col, -1), kv_spec(v_col, 0), kv_spec(v_col, 1),
                  pl.BlockSpec((C_KV_HEADS, G, blk, 3 * blk), lambda b, n, sref: (0, 0, 0, 0)),
                  pl.BlockSpec(mask.shape, lambda b, n, sref: (0, 0, 0))],
        out_specs=pl.BlockSpec((blk, C_HEADS * C_DH), lambda b, n, sref: (b * nb + n, 0)))
    return pl.pallas_call(
        _win_kernel,
        out_shape=jax.ShapeDtypeStruct((T, C_HEADS * C_DH), BF16),
        grid_spec=grid_spec,
        compiler_params=_params("parallel", "arbitrary"),
        name="window_gqa",
    )(sink, zc, zc, zc, zc, zc, zc, zc, bias, mask)


def _merge_kernel(x_ref, ya_ref, yb_ref, yc_ref, ga_ref, gb_ref, gc_ref,
                  wa_ref, wb_ref, wc_ref, wo_ref, o_ref):
    def branch(y_ref, gate_ref, w_ref):
        return jax.nn.sigmoid(gate_ref[...]) * jnp.dot(y_ref[...], w_ref[...], preferred_element_type=F32)

    merged = (branch(ya_ref, ga_ref, wa_ref) + branch(yb_ref, gb_ref, wb_ref)
              + branch(yc_ref, gc_ref, wc_ref))
    o_ref[...] = x_ref[...] + jnp.dot(merged.astype(BF16), wo_ref[...], preferred_element_type=F32)


def _merge(x, ya, yb, yc, gates, wa, wb, wc, wo, tm):
    T, D = x.shape
    row = lambda i: (i, 0)
    return pl.pallas_call(
        _merge_kernel,
        out_shape=jax.ShapeDtypeStruct((T, D), F32),
        grid=(T // tm,),
        in_specs=[pl.BlockSpec((tm, D), row),
                  pl.BlockSpec((tm, ya.shape[1]), row),
                  pl.BlockSpec((tm, yb.shape[1]), row),
                  pl.BlockSpec((tm, yc.shape[1]), row),
                  pl.BlockSpec((tm, D), lambda i: (i, 0)),
                  pl.BlockSpec((tm, D), lambda i: (i, 1)),
                  pl.BlockSpec((tm, D), lambda i: (i, 2)),
                  _resident(wa.shape), _resident(wb.shape), _resident(wc.shape), _resident(wo.shape)],
        out_specs=pl.BlockSpec((tm, D), row),
        compiler_params=_params("parallel"),
        name="merge",
    )(x, ya, yb, yc, gates, gates, gates, wa, wb, wc, wo)


def _cross_kernel(x_ref, g_ref, kv_ref, wq_ref, wo_ref, o_ref):
    x = x_ref[...]
    h = _rms(x, g_ref[...]).astype(BF16)
    q = jnp.dot(h, wq_ref[...], preferred_element_type=F32).astype(BF16)
    kv = kv_ref[0]
    width = X_HEADS * X_DH
    outs = []
    for hd in range(X_HEADS):
        sl = slice(hd * X_DH, (hd + 1) * X_DH)
        k = kv[:, sl]
        v = kv[:, width + hd * X_DH: width + (hd + 1) * X_DH]
        s = lax.dot_general(q[:, sl], k, NT_DIMS, preferred_element_type=F32) * (X_DH ** -0.5)
        p = jnp.exp(s - jnp.max(s, axis=-1, keepdims=True))
        p = p / jnp.sum(p, axis=-1, keepdims=True)
        outs.append(jnp.dot(p.astype(BF16), v, preferred_element_type=F32).astype(BF16))
    o = jnp.concatenate(outs, axis=-1)
    o_ref[...] = x + jnp.dot(o, wo_ref[...], preferred_element_type=F32)


def _cross(x, g, kv, wq, wo, seq, tm):
    T, D = x.shape
    nseq = seq // tm
    row = lambda i: (i, 0)
    return pl.pallas_call(
        _cross_kernel,
        out_shape=jax.ShapeDtypeStruct((T, D), F32),
        grid=(T // tm,),
        in_specs=[pl.BlockSpec((tm, D), row),
                  _resident((1, D)),
                  pl.BlockSpec((1,) + kv.shape[1:], lambda i: (i // nseq, 0, 0)),
                  _resident(wq.shape), _resident(wo.shape)],
        out_specs=pl.BlockSpec((tm, D), row),
        compiler_params=_params("parallel"),
        name="cross_attn",
    )(x, g.reshape(1, D), kv, wq, wo)


def _swiglu_kernel(x_ref, g_ref, w1_ref, w3_ref, w2_ref, gf_ref, o_ref, *, final_norm):
    x = x_ref[...]
    h = _rms(x, g_ref[...]).astype(BF16)
    a = jnp.dot(h, w1_ref[...], preferred_element_type=F32)
    b = jnp.dot(h, w3_ref[...], preferred_element_type=F32)
    u = (jax.nn.silu(a) * b).astype(BF16)
    y = x + jnp.dot(u, w2_ref[...], preferred_element_type=F32)
    if final_norm:
        y = _rms(y, gf_ref[...])
    o_ref[...] = y


def _swiglu(x, g, w1, w3, w2, g_final, final_norm, tm):
    T, D = x.shape
    row = lambda i: (i, 0)
    return pl.pallas_call(
        functools.partial(_swiglu_kernel, final_norm=final_norm),
        out_shape=jax.ShapeDtypeStruct((T, D), F32),
        grid=(T // tm,),
        in_specs=[pl.BlockSpec((tm, D), row), _resident((1, D)),
                  _resident(w1.shape), _resident(w3.shape), _resident(w2.shape), _resident((1, D))],
        out_specs=pl.BlockSpec((tm, D), row),
        compiler_params=_params("parallel"),
        name="swiglu",
    )(x, g.reshape(1, D), w1, w3, w2, g_final.reshape(1, D))


def _pad_cols(w, width):
    return jnp.pad(w, ((0, 0), (0, width - w.shape[1])))


def _group_cols(w, heads, width):
    K = w.shape[0]
    w = w.reshape(K, heads, -1)
    return jnp.pad(w, ((0, 0), (0, 0), (0, width - w.shape[2]))).reshape(K, heads * width)


def _group_rows(w, heads, width):
    N = w.shape[1]
    w = w.reshape(heads, -1, N)
    return jnp.pad(w, ((0, 0), (0, width - w.shape[1]), (0, 0))).reshape(heads * width, N)


def _split_in_proj(w):
    o = 0
    parts = []
    for n in (A_Q_RANK, A_KV_RANK, A_ROPE, 3 * B_HEADS * B_DK + 2 * B_HEADS * B_DV,
              C_HEADS * C_DH, C_KV_HEADS * C_DH, C_KV_HEADS * C_DH, 3 * w.shape[0]):
        parts.append(w[:, o:o + n])
        o += n
    cq, ckv, kr, wb, wcq, wck, wcv, wg = parts
    half = A_ROPE // 2
    wa = jnp.concatenate([cq, ckv, _pad_cols(kr[:, :half], LANES), _pad_cols(kr[:, half:], LANES)], axis=1)
    wc = jnp.concatenate([_group_cols(wcq, C_HEADS, LANES), _group_cols(wck, C_KV_HEADS, LANES),
                          _group_cols(wcv, C_KV_HEADS, LANES)], axis=1)
    return [t.astype(BF16) for t in (wa, wb, wc, wg)]


def _mla_weights(wuq, wukv):
    half = A_ROPE // 2
    wq = wuq.reshape(A_Q_RANK, A_HEADS, A_NOPE + A_ROPE)
    wqn = _group_cols(wq[:, :, :A_NOPE].reshape(A_Q_RANK, -1), A_HEADS, LANES)
    wqr = jnp.concatenate([wq[:, :, A_NOPE:A_NOPE + half].reshape(A_Q_RANK, -1),
                           wq[:, :, A_NOPE + half:].reshape(A_Q_RANK, -1)], axis=1)
    wkv = wukv.reshape(A_KV_RANK, A_HEADS, A_NOPE + A_V)
    wkn = _group_cols(wkv[:, :, :A_NOPE].reshape(A_KV_RANK, -1), A_HEADS, LANES)
    wv = _group_cols(wkv[:, :, A_NOPE:].reshape(A_KV_RANK, -1), A_HEADS, LANES)
    return [t.astype(BF16) for t in (wqn, wqr, wkn, wv)]


def _mla_placement():
    half = A_ROPE // 2
    src = jnp.arange(LANES)[:, None]
    dst = jnp.arange(A_HEADS * LANES)[None, :]
    head, lane = dst // LANES, dst % LANES
    mats = []
    for per_head, base in ((True, A_NOPE), (True, A_NOPE + half), (False, A_NOPE), (False, A_NOPE + half)):
        j = lane - base
        want = head * half + j if per_head else j
        mats.append(((j >= 0) & (j < half) & (src == want)).astype(BF16))
    ones_row = (lane == A_V).astype(F32)
    return jnp.stack(mats), ones_row


def kernel(x, mem, w_in, g_mix, a_gq, a_gkv, a_wuq, a_wukv, b_lb, b_gout, c_sink, rel_bias,
           w_br_a, w_br_b, w_br_c, w_out, g_x, g_mem, x_wq, x_wkv, x_wo, g_ffn,
           f_w1, f_w3, f_w2, g_final):
    Bsz, S, D = x.shape
    depth = w_in.shape[0]
    T = Bsz * S
    M = mem.shape[1]
    tm = ROW_TILE
    half = A_ROPE // 2

    inv = ROPE_THETA ** (-jnp.arange(half, dtype=F32) / half)
    ang = jnp.arange(S, dtype=jnp.int32).astype(F32)[:, None] * inv[None, :]
    cos = jnp.tile(jnp.cos(ang), (1, LANES // half))
    sin = jnp.tile(jnp.sin(ang), (1, LANES // half))
    place, ones_row = _mla_placement()

    sm = jax.nn.softmax(b_lb.astype(F32), axis=1)
    lower = jnp.cumsum(sm, axis=1) - sm[:, :1]

    span = 3 * C_BLOCK
    rel = jnp.arange(span)[None, :] - C_BLOCK - jnp.arange(C_BLOCK)[:, None]
    G = C_HEADS // C_KV_HEADS
    onehot = (_t5_bucket(rel)[None] == jnp.arange(REL_BUCKETS)[:, None, None]).astype(F32)
    bias = jnp.einsum('nh,nqs->hqs', rel_bias.astype(F32), onehot,
                      precision=lax.Precision.HIGHEST).reshape(C_KV_HEADS, G, C_BLOCK, span)
    col = jnp.arange(span)[None, :]
    band = jnp.abs(rel) <= C_WINDOW
    win_mask = jnp.stack([band, band & (col >= C_BLOCK), band & (col < 2 * C_BLOCK)]).astype(F32)
    win_mask = jnp.tile(win_mask, (1, G, 1))

    xt = x.reshape(T, D)
    mem2 = mem.reshape(Bsz * M, D)
    for l in range(depth):
        za, zb, zc, zg = _rms_proj(xt, g_mix[l], _split_in_proj(w_in[l]), (F32, F32, BF16, F32), IN_PROJ_TILE)

        qa, ka, va = _mla_prep(za, a_gq[l], a_gkv[l], *_mla_weights(a_wuq[l], a_wukv[l]),
                               place, ones_row, cos, sin, S, tm)
        shp = (Bsz, S, A_HEADS * LANES)
        ya = _mla_attn(qa.reshape(shp), ka.reshape(shp), va.reshape(shp),
                       MLA_TQ, MLA_SUB, MLA_STEPS).reshape(T, -1)

        of = _hgrn_direction(zb, lower[0, l], None, None, S, reverse=False)
        yb = _hgrn_direction(zb, lower[1, l], of, b_gout[l], S, reverse=True)

        yc = _window_gqa(zc, bias, win_mask, c_sink[l].astype(F32), S)

        xt = _merge(xt, ya, yb, yc, zg,
                    _group_rows(w_br_a[l], A_HEADS, LANES).astype(BF16), w_br_b[l].astype(BF16),
                    w_br_c[l].astype(BF16), w_out[l].astype(BF16), tm)

        (kvm,) = _rms_proj(mem2, g_mem[l], [x_wkv[l].astype(BF16)], (BF16,), Bsz * M)
        xt = _cross(xt, g_x[l], kvm.reshape(Bsz, M, -1), x_wq[l].astype(BF16), x_wo[l].astype(BF16), S, tm)

        xt = _swiglu(xt, g_ffn[l], f_w1[l].astype(BF16), f_w3[l].astype(BF16), f_w2[l].astype(BF16),
                     g_final, l == depth - 1, tm)
    return xt.reshape(Bsz, S, D)
```

```python
import functools
import math

import jax
import jax.numpy as jnp
from jax import lax
from jax.experimental import pallas as pl
from jax.experimental.pallas import tpu as pltpu

F32 = jnp.float32
BF16 = jnp.bfloat16

EPS = 1e-6
MASK_VALUE = -1e30
TINY = 1e-30
ROPE_THETA = 10000.0

A_HEADS, A_NOPE, A_ROPE, A_V = 8, 64, 32, 64
A_Q_RANK, A_KV_RANK = 384, 256
B_HEADS, B_DK, B_DV = 8, 128, 64
C_HEADS, C_KV_HEADS, C_DH, C_WINDOW, C_BLOCK = 8, 2, 64, 128, 128
REL_BUCKETS, REL_MAX_DIST = 32, 128
X_HEADS, X_DH = 4, 256

LANES = 128
VMEM_LIMIT = 56 * 1024 * 1024

ROW_TILE = 512
IN_PROJ_TILE = 256
MLA_TQ = 1024
MLA_SUB = 256
MLA_STEPS = 32
MLA_TCHUNK = 1024
MLA_SLAB = 16
MLA_VROWS = 80
HGRN_CHUNK = 32
HGRN_BLOCK = 1024

NT_DIMS = (((1,), (1,)), ((), ()))


def _params(*sem):
    return pltpu.CompilerParams(dimension_semantics=sem, vmem_limit_bytes=VMEM_LIMIT)


def _rms(x, g):
    return x * lax.rsqrt(jnp.mean(x * x, axis=-1, keepdims=True) + EPS) * g


def _resident(shape):
    return pl.BlockSpec(shape, lambda *_: (0,) * len(shape), pipeline_mode=pl.Buffered(1))


def _rms_proj_kernel(*refs, n_out):
    x_ref, g_ref = refs[:2]
    w_refs = refs[2:2 + n_out]
    o_refs = refs[2 + n_out:]
    h = _rms(x_ref[...], g_ref[...]).astype(BF16)
    for w_ref, o_ref in zip(w_refs, o_refs):
        o_ref[...] = jnp.dot(h, w_ref[...], preferred_element_type=F32).astype(o_ref.dtype)


def _rms_proj(x, g, weights, out_dtypes, tm):
    T, D = x.shape
    row = lambda i: (i, 0)
    return pl.pallas_call(
        functools.partial(_rms_proj_kernel, n_out=len(weights)),
        out_shape=tuple(jax.ShapeDtypeStruct((T, w.shape[1]), dt) for w, dt in zip(weights, out_dtypes)),
        grid=(T // tm,),
        in_specs=[pl.BlockSpec((tm, D), row), _resident((1, D))] + [_resident(w.shape) for w in weights],
        out_specs=tuple(pl.BlockSpec((tm, w.shape[1]), row) for w in weights),
        compiler_params=_params("parallel"),
        name="rms_proj",
    )(x, g.reshape(1, D), *weights)


def _mla_prep_kernel(za_ref, gq_ref, gkv_ref, wqn_ref, wqr_ref, wkn_ref, wv_ref, place_ref, ones_ref,
                     cos_ref, sin_ref, q_ref, k_ref, v_ref, *, q_scale):
    za = za_ref[...]
    cq = za[:, :A_Q_RANK]
    ckv = za[:, A_Q_RANK:A_Q_RANK + A_KV_RANK]
    kr1 = za[:, 640:768]
    kr2 = za[:, 768:896]
    cos = cos_ref[...]
    sin = sin_ref[...]

    def place(x, idx):
        return jnp.dot(x.astype(BF16), place_ref[idx], preferred_element_type=F32)

    hq = _rms(cq, gq_ref[...]).astype(BF16)
    qr = jnp.dot(hq, wqr_ref[...], preferred_element_type=F32)
    q1 = qr[:, :LANES]
    q2 = qr[:, LANES:]
    qn = jnp.dot(hq, wqn_ref[...], preferred_element_type=F32)
    q_ref[...] = (qn * q_scale + place((q1 * cos - q2 * sin) * q_scale, 0)
                  + place((q1 * sin + q2 * cos) * q_scale, 1)).astype(BF16)

    hkv = _rms(ckv, gkv_ref[...]).astype(BF16)
    kn = jnp.dot(hkv, wkn_ref[...], preferred_element_type=F32)
    k_ref[...] = (kn + place(kr1 * cos - kr2 * sin, 2) + place(kr1 * sin + kr2 * cos, 3)).astype(BF16)
    v_ref[...] = (jnp.dot(hkv, wv_ref[...], preferred_element_type=F32) + ones_ref[...]).astype(BF16)


def _mla_prep(za, gq, gkv, wqn, wqr, wkn, wv, place, ones_row, cos, sin, seq, tm):
    T = za.shape[0]
    width = A_HEADS * LANES
    nseq = seq // tm
    q_scale = (A_NOPE + A_ROPE) ** -0.5 * math.log2(math.e)
    row = lambda i: (i, 0)
    pos = lambda i: (i % nseq, 0)
    out = jax.ShapeDtypeStruct((T, width), BF16)
    return pl.pallas_call(
        functools.partial(_mla_prep_kernel, q_scale=q_scale),
        out_shape=(out, out, out),
        grid=(T // tm,),
        in_specs=[pl.BlockSpec((tm, za.shape[1]), row),
                  _resident((1, A_Q_RANK)), _resident((1, A_KV_RANK)),
                  _resident(wqn.shape), _resident(wqr.shape), _resident(wkn.shape), _resident(wv.shape),
                  _resident(place.shape), _resident(ones_row.shape),
                  pl.BlockSpec((tm, LANES), pos), pl.BlockSpec((tm, LANES), pos)],
        out_specs=(pl.BlockSpec((tm, width), row),) * 3,
        compiler_params=_params("parallel"),
        name="mla_prep",
    )(za, gq.reshape(1, -1), gkv.reshape(1, -1), wqn, wqr, wkn, wv, place, ones_row, cos, sin)


def _mla_attn_kernel(q_ref, k_ref, v_ref, o_ref, vt_ref, s_buf, p_buf, *, sub, steps):
    tq = q_ref.shape[1]
    seq = k_ref.shape[1]
    n_sub = seq // sub

    @pl.when(pl.program_id(2) == 0)
    def _():
        def tr(c, carry):
            off = pl.multiple_of(c * MLA_TCHUNK, MLA_TCHUNK)
            vt = v_ref[0, pl.ds(off, MLA_TCHUNK), :].astype(F32).T
            vt_ref[:, pl.ds(off, MLA_TCHUNK)] = vt[:MLA_VROWS].astype(BF16)
            return carry
        lax.fori_loop(0, seq // MLA_TCHUNK, tr, 0)

    qt = q_ref[0].astype(F32).T.astype(BF16)

    def qk(j, slot):
        off = pl.multiple_of(jnp.minimum(j, n_sub - 1) * sub, sub)
        s = jnp.dot(k_ref[0, pl.ds(off, sub), :], qt, preferred_element_type=F32)
        s_buf[slot] = s
        return jnp.max(s, axis=0, keepdims=True)

    def step(j, slot, m, acc, cmax):
        cmax_next = qk(j + 1, 1 - slot)
        m_new = jnp.maximum(m, cmax)
        alpha = jnp.exp2(m - m_new)
        mb = jnp.broadcast_to(m_new, (MLA_SLAB, tq))
        for r in range(sub // MLA_SLAB):
            rows = pl.ds(r * MLA_SLAB, MLA_SLAB)
            p_buf[slot, rows, :] = jnp.exp2(s_buf[slot, rows, :] - mb).astype(BF16)
        off = pl.multiple_of(j * sub, sub)
        acc = alpha * acc + jnp.dot(vt_ref[:, pl.ds(off, sub)], p_buf[slot], preferred_element_type=F32)
        return m_new, acc, cmax_next

    def body(i, carry):
        for u in range(steps):
            carry = step(steps * i + u, u % 2, *carry)
        return carry

    m0 = jnp.full((1, tq), MASK_VALUE, F32)
    acc0 = jnp.zeros((MLA_VROWS, tq), F32)
    _, acc, _ = lax.fori_loop(0, n_sub // steps, body, (m0, acc0, qk(0, 0)))
    ot = acc[:A_V] / acc[A_V:A_V + 1]
    ot = jnp.concatenate([ot, jnp.zeros((LANES - A_V, tq), F32)], axis=0)
    o_ref[0] = ot.T.astype(o_ref.dtype)


def _mla_attn(q, k, v, tq, sub, steps):
    Bsz, S, width = q.shape
    assert S % tq == 0 and S % (sub * steps) == 0 and steps % 2 == 0 and S % MLA_TCHUNK == 0
    return pl.pallas_call(
        functools.partial(_mla_attn_kernel, sub=sub, steps=steps),
        out_shape=jax.ShapeDtypeStruct((Bsz, S, width), BF16),
        grid=(Bsz, width // LANES, S // tq),
        in_specs=[pl.BlockSpec((1, tq, LANES), lambda b, h, i: (b, i, h)),
                  pl.BlockSpec((1, S, LANES), lambda b, h, i: (b, 0, h)),
                  pl.BlockSpec((1, S, LANES), lambda b, h, i: (b, 0, h))],
        out_specs=pl.BlockSpec((1, tq, LANES), lambda b, h, i: (b, i, h)),
        scratch_shapes=[pltpu.VMEM((MLA_VROWS, S), BF16),
                        pltpu.VMEM((2, sub, tq), F32), pltpu.VMEM((2, sub, tq), BF16)],
        compiler_params=_params("parallel", "parallel", "arbitrary"),
        name="mla_attn",
    )(q, k, v)


def _chunk_cumsum(x, chunk, reverse):
    n, w = x.shape
    sub = 8
    per = chunk // sub
    x3 = x.reshape(n // sub, sub, w)
    row = lax.broadcasted_iota(jnp.int32, x3.shape, 1)
    d = 1
    while d < sub:
        if reverse:
            x3 = x3 + jnp.where(row < sub - d, pltpu.roll(x3, sub - d, 1), 0.0)
        else:
            x3 = x3 + jnp.where(row >= d, pltpu.roll(x3, d, 1), 0.0)
        d *= 2
    xc = x3.reshape(n // chunk, chunk, w)
    parts = [xc[:, i * sub:(i + 1) * sub, :] for i in range(per)]
    edge = 0 if reverse else sub - 1
    carry = None
    for i in (range(per - 1, -1, -1) if reverse else range(per)):
        total = parts[i][:, edge:edge + 1, :]
        if carry is not None:
            parts[i] = parts[i] + carry
            carry = carry + total
        else:
            carry = total
    return jnp.concatenate(parts, axis=1).reshape(n, w)


def _hgrn_kernel(*refs, reverse, finalize):
    if finalize:
        (q_ref, z_ref, i_ref, lb_ref, prev_ref, og_ref, gout_ref, o_ref, st_ref, sall_ref) = refs
    else:
        (q_ref, z_ref, i_ref, lb_ref, o_ref, st_ref, sall_ref) = refs
    C = HGRN_CHUNK
    tb = q_ref.shape[0]
    nc = tb // C
    half = C // 2

    @pl.when(pl.program_id(2) == 0)
    def _():
        st_ref[...] = jnp.zeros_like(st_ref)

    q = q_ref[...]
    z = z_ref[...]
    lb = lb_ref[...]
    v = i_ref[...]
    sg = jax.nn.sigmoid(z)
    f = lb + (1.0 - lb) * sg
    key = (1.0 - lb) * (1.0 - sg)
    b = _chunk_cumsum(jnp.log(jnp.maximum(f, TINY)), C, reverse)

    b3 = b.reshape(nc, C, 2 * B_DK)
    if reverse:
        ref3 = b3[:, half:half + 1, :]
        tot3 = b3[:, 0:1, :]
    else:
        ref3 = b3[:, half - 1:half, :]
        tot3 = b3[:, C - 1:C, :]
    qs = q.reshape(nc, C, 2 * B_DK) * jnp.exp(b3 - ref3)
    ks = key.reshape(nc, C, 2 * B_DK) * jnp.exp(ref3 - b3)
    q_dec = (qs * jnp.exp(ref3)).astype(BF16)
    k_end = (ks * jnp.exp(tot3 - ref3)).astype(BF16)
    dec = jnp.exp(tot3)
    qs = qs.astype(BF16)
    ks = ks.astype(BF16)
    v3 = v.astype(BF16).reshape(nc, C, 2 * B_DV)

    t_idx = lax.broadcasted_iota(jnp.int32, (nc, C, C), 1)
    s_idx = lax.broadcasted_iota(jnp.int32, (nc, C, C), 2)
    causal = (s_idx >= t_idx) if reverse else (s_idx <= t_idx)

    lane_v = lax.broadcasted_iota(jnp.int32, v3.shape, 2)
    v_own = [jnp.where(lane_v < B_DV, v3, jnp.zeros_like(v3)), jnp.where(lane_v >= B_DV, v3, jnp.zeros_like(v3))]
    upd = sum(jnp.einsum('csv,csk->cvk', v_own[h], k_end[:, :, h * B_DK:(h + 1) * B_DK],
                         preferred_element_type=F32) for h in range(2))
    row_v = lax.broadcasted_iota(jnp.int32, (2 * B_DV, B_DK), 0)
    state = st_ref[...]
    for c in (range(nc - 1, -1, -1) if reverse else range(nc)):
        sall_ref[c] = state.astype(BF16)
        decay = jnp.where(row_v < B_DV, dec[c, :, :B_DK], dec[c, :, B_DK:])
        state = state * decay + upd[c]
    st_ref[...] = state

    outs = []
    for h in range(2):
        sl = slice(h * B_DK, (h + 1) * B_DK)
        scores = jnp.einsum('ctk,csk->cts', qs[:, :, sl], ks[:, :, sl], preferred_element_type=F32)
        scores = jnp.where(causal, scores, 0.0).astype(BF16)
        o_h = (jnp.einsum('cts,csv->ctv', scores, v3, preferred_element_type=F32)
               + jnp.einsum('ctk,cvk->ctv', q_dec[:, :, sl], sall_ref[...], preferred_element_type=F32))
        outs.append(o_h.reshape(tb, 2 * B_DV))
    lane = lax.broadcasted_iota(jnp.int32, (tb, 2 * B_DV), 1)
    first = lane < B_DV
    o = jnp.where(first, outs[0], outs[1])
    if not finalize:
        o_ref[...] = o
        return
    o = o + prev_ref[...]
    sq = o * o
    ms0 = jnp.sum(jnp.where(first, sq, 0.0), axis=-1, keepdims=True)
    ms1 = jnp.sum(jnp.where(first, 0.0, sq), axis=-1, keepdims=True)
    ms = jnp.where(first, ms0, ms1) * (1.0 / B_DV)
    y = o * lax.rsqrt(ms + EPS) * gout_ref[...]
    o_ref[...] = (y * jax.nn.silu(og_ref[...])).astype(o_ref.dtype)


def _hgrn_direction(zb, lb, prev, gout, seq, reverse):
    T = zb.shape[0]
    Bsz = T // seq
    tb = HGRN_BLOCK
    nblk = seq // tb
    pairs = B_HEADS // 2
    dk2, dv2 = 2 * B_DK, 2 * B_DV
    finalize = prev is not None

    def row(b, p, i):
        return b * nblk + (nblk - 1 - i if reverse else i)

    z_base = (2 if reverse else 1) * (B_HEADS * B_DK) // dk2
    i_base = 3 * (B_HEADS * B_DK) // dv2
    g_base = i_base + (B_HEADS * B_DV) // dv2
    in_specs = [pl.BlockSpec((tb, dk2), lambda b, p, i: (row(b, p, i), p)),
                pl.BlockSpec((tb, dk2), lambda b, p, i: (row(b, p, i), z_base + p)),
                pl.BlockSpec((tb, dv2), lambda b, p, i: (row(b, p, i), i_base + p)),
                pl.BlockSpec((1, dk2), lambda b, p, i: (0, p))]
    args = [zb, zb, zb, lb.reshape(1, -1)]
    if finalize:
        in_specs += [pl.BlockSpec((tb, dv2), lambda b, p, i: (row(b, p, i), p)),
                     pl.BlockSpec((tb, dv2), lambda b, p, i: (row(b, p, i), g_base + p)),
                     pl.BlockSpec((1, dv2), lambda b, p, i: (0, 0))]
        args += [prev, zb, jnp.tile(gout, 2).reshape(1, dv2)]
    return pl.pallas_call(
        functools.partial(_hgrn_kernel, reverse=reverse, finalize=finalize),
        out_shape=jax.ShapeDtypeStruct((T, B_HEADS * B_DV), BF16 if finalize else F32),
        grid=(Bsz, pairs, nblk),
        in_specs=in_specs,
        out_specs=pl.BlockSpec((tb, dv2), lambda b, p, i: (row(b, p, i), p)),
        scratch_shapes=[pltpu.VMEM((dv2, B_DK), F32),
                        pltpu.VMEM((tb // HGRN_CHUNK, dv2, B_DK), BF16)],
        compiler_params=_params("parallel", "parallel", "arbitrary"),
        name="hgrn_bwd" if reverse else "hgrn_fwd",
    )(*args)


def _t5_bucket(rel):
    nb = REL_BUCKETS // 2
    max_exact = nb // 2
    ret = (rel > 0).astype(jnp.int32) * nb
    n = jnp.abs(rel)
    large = max_exact + (jnp.log(jnp.maximum(n, 1).astype(F32) / max_exact)
                         / math.log(REL_MAX_DIST / max_exact) * (nb - max_exact)).astype(jnp.int32)
    large = jnp.minimum(large, nb - 1)
    return ret + jnp.where(n < max_exact, n, large)


def _win_kernel(sink_ref, q_ref, kp_ref, kc_ref, kn_ref, vp_ref, vc_ref, vn_ref, bias_ref, mask_ref, o_ref):
    G = C_HEADS // C_KV_HEADS
    blk = C_BLOCK
    n = pl.program_id(1)
    last = pl.num_programs(1) - 1
    rows, span = G * blk, 3 * blk

    which = jnp.where(n == 0, 1, jnp.where(n == last, 2, 0))
    valid = mask_ref[which] != 0.0
    g_idx = lax.broadcasted_iota(jnp.int32, (rows, 1), 0) // blk

    def scores(kvh):
        grp = slice(kvh * LANES, (kvh + 1) * LANES)
        q = jnp.concatenate([q_ref[:, (kvh * G + g) * LANES:(kvh * G + g + 1) * LANES] for g in range(G)],
                            axis=0)
        k = jnp.concatenate([kp_ref[:, grp], kc_ref[:, grp], kn_ref[:, grp]], axis=0)
        s = lax.dot_general(q, k, NT_DIMS, preferred_element_type=F32) * (C_DH ** -0.5)
        s = s + bias_ref[kvh].reshape(rows, span)
        return jnp.where(valid, s, MASK_VALUE)

    all_scores = [scores(kvh) for kvh in range(C_KV_HEADS)]
    for kvh, s in enumerate(all_scores):
        grp = slice(kvh * LANES, (kvh + 1) * LANES)
        v = jnp.concatenate([vp_ref[:, grp], vc_ref[:, grp], vn_ref[:, grp]], axis=0)
        sink = jnp.zeros((rows, 1), F32)
        for g in range(G):
            sink = jnp.where(g_idx == g, sink_ref[kvh * G + g], sink)
        m = jnp.maximum(jnp.max(s, axis=-1, keepdims=True), sink)
        p = jnp.exp(s - m)
        p = p / (jnp.sum(p, axis=-1, keepdims=True) + jnp.exp(sink - m))
        o = jnp.dot(p.astype(BF16), v, preferred_element_type=F32)
        for pair in range(G // 2):
            first = o[(2 * pair) * blk:(2 * pair + 1) * blk]
            second = pltpu.roll(o[(2 * pair + 1) * blk:(2 * pair + 2) * blk], C_DH, 1)
            col = (kvh * G // 2 + pair) * LANES
            o_ref[:, col:col + LANES] = (first + second).astype(o_ref.dtype)


def _window_gqa(zc, bias, mask, sink, seq):
    T = zc.shape[0]
    blk = C_BLOCK
    nb = seq // blk
    assert nb >= 2
    qw = C_HEADS * LANES
    kw = C_KV_HEADS * LANES
    G = C_HEADS // C_KV_HEADS

    def kv_spec(col, shift):
        return pl.BlockSpec((blk, kw), lambda b, n, sref: (b * nb + jnp.clip(n + shift, 0, nb - 1), col))

    k_col, v_col = qw // kw, qw // kw + 1
    grid_spec = pltpu.PrefetchScalarGridSpec(
        num_scalar_prefetch=1,
        grid=(T // seq, nb),
        in_specs=[pl.BlockSpec((blk, qw), lambda b, n, sref: (b * nb + n, 0)),
                  kv_spec(k_col, -1), kv_spec(k_col, 0), kv_spec(k_col, 1),
                  kv_spec(v_col, -1), kv_spec(v_col, 0), kv_spec(v_col, 1),
                  pl.BlockSpec((C_KV_HEADS, G, blk, 3 * blk), lambda b, n, sref: (0, 0, 0, 0)),
                  pl.BlockSpec(mask.shape, lambda b, n, sref: (0, 0, 0))],
        out_specs=pl.BlockSpec((blk, C_HEADS * C_DH), lambda b, n, sref: (b * nb + n, 0)))
    return pl.pallas_call(
        _win_kernel,
        out_shape=jax.ShapeDtypeStruct((T, C_HEADS * C_DH), BF16),
        grid_spec=grid_spec,
        compiler_params=_params("parallel", "arbitrary"),
        name="window_gqa",
    )(sink, zc, zc, zc, zc, zc, zc, zc, bias, mask)


def _merge_kernel(x_ref, ya_ref, yb_ref, yc_ref, ga_ref, gb_ref, gc_ref,
                  wa_ref, wb_ref, wc_ref, wo_ref, o_ref):
    def branch(y_ref, gate_ref, w_ref):
        return jax.nn.sigmoid(gate_ref[...]) * jnp.dot(y_ref[...], w_ref[...], preferred_element_type=F32)

    merged = (branch(ya_ref, ga_ref, wa_ref) + branch(yb_ref, gb_ref, wb_ref)
              + branch(yc_ref, gc_ref, wc_ref))
    o_ref[...] = x_ref[...] + jnp.dot(merged.astype(BF16), wo_ref[...], preferred_element_type=F32)


def _merge(x, ya, yb, yc, gates, wa, wb, wc, wo, tm):
    T, D = x.shape
    row = lambda i: (i, 0)
    return pl.pallas_call(
        _merge_kernel,
        out_shape=jax.ShapeDtypeStruct((T, D), F32),
        grid=(T // tm,),
        in_specs=[pl.BlockSpec((tm, D), row),
                  pl.BlockSpec((tm, ya.shape[1]), row),
                  pl.BlockSpec((tm, yb.shape[1]), row),
                  pl.BlockSpec((tm, yc.shape[1]), row),
                  pl.BlockSpec((tm, D), lambda i: (i, 0)),
                  pl.BlockSpec((tm, D), lambda i: (i, 1)),
                  pl.BlockSpec((tm, D), lambda i: (i, 2)),
                  _resident(wa.shape), _resident(wb.shape), _resident(wc.shape), _resident(wo.shape)],
        out_specs=pl.BlockSpec((tm, D), row),
        compiler_params=_params("parallel"),
        name="merge",
    )(x, ya, yb, yc, gates, gates, gates, wa, wb, wc, wo)


def _cross_kernel(x_ref, g_ref, kv_ref, wq_ref, wo_ref, o_ref):
    x = x_ref[...]
    h = _rms(x, g_ref[...]).astype(BF16)
    q = jnp.dot(h, wq_ref[...], preferred_element_type=F32).astype(BF16)
    kv = kv_ref[0]
    width = X_HEADS * X_DH
    outs = []
    for hd in range(X_HEADS):
        sl = slice(hd * X_DH, (hd + 1) * X_DH)
        k = kv[:, sl]
        v = kv[:, width + hd * X_DH: width + (hd + 1) * X_DH]
        s = lax.dot_general(q[:, sl], k, NT_DIMS, preferred_element_type=F32) * (X_DH ** -0.5)
        p = jnp.exp(s - jnp.max(s, axis=-1, keepdims=True))
        p = p / jnp.sum(p, axis=-1, keepdims=True)
        outs.append(jnp.dot(p.astype(BF16), v, preferred_element_type=F32).astype(BF16))
    o = jnp.concatenate(outs, axis=-1)
    o_ref[...] = x + jnp.dot(o, wo_ref[...], preferred_element_type=F32)


def _cross(x, g, kv, wq, wo, seq, tm):
    T, D = x.shape
    nseq = seq // tm
    row = lambda i: (i, 0)
    return pl.pallas_call(
        _cross_kernel,
        out_shape=jax.ShapeDtypeStruct((T, D), F32),
        grid=(T // tm,),
        in_specs=[pl.BlockSpec((tm, D), row),
                  _resident((1, D)),
                  pl.BlockSpec((1,) + kv.shape[1:], lambda i: (i // nseq, 0, 0)),
                  _resident(wq.shape), _resident(wo.shape)],
        out_specs=pl.BlockSpec((tm, D), row),
        compiler_params=_params("parallel"),
        name="cross_attn",
    )(x, g.reshape(1, D), kv, wq, wo)


def _swiglu_kernel(x_ref, g_ref, w1_ref, w3_ref, w2_ref, gf_ref, o_ref, *, final_norm):
    x = x_ref[...]
    h = _rms(x, g_ref[...]).astype(BF16)
    a = jnp.dot(h, w1_ref[...], preferred_element_type=F32)
    b = jnp.dot(h, w3_ref[...], preferred_element_type=F32)
    u = (jax.nn.silu(a) * b).astype(BF16)
    y = x + jnp.dot(u, w2_ref[...], preferred_element_type=F32)
    if final_norm:
        y = _rms(y, gf_ref[...])
    o_ref[...] = y


def _swiglu(x, g, w1, w3, w2, g_final, final_norm, tm):
    T, D = x.shape
    row = lambda i: (i, 0)
    return pl.pallas_call(
        functools.partial(_swiglu_kernel, final_norm=final_norm),
        out_shape=jax.ShapeDtypeStruct((T, D), F32),
        grid=(T // tm,),
        in_specs=[pl.BlockSpec((tm, D), row), _resident((1, D)),
                  _resident(w1.shape), _resident(w3.shape), _resident(w2.shape), _resident((1, D))],
        out_specs=pl.BlockSpec((tm, D), row),
        compiler_params=_params("parallel"),
        name="swiglu",
    )(x, g.reshape(1, D), w1, w3, w2, g_final.reshape(1, D))


def _pad_cols(w, width):
    return jnp.pad(w, ((0, 0), (0, width - w.shape[1])))


def _group_cols(w, heads, width):
    K = w.shape[0]
    w = w.reshape(K, heads, -1)
    return jnp.pad(w, ((0, 0), (0, 0), (0, width - w.shape[2]))).reshape(K, heads * width)


def _group_rows(w, heads, width):
    N = w.shape[1]
    w = w.reshape(heads, -1, N)
    return jnp.pad(w, ((0, 0), (0, width - w.shape[1]), (0, 0))).reshape(heads * width, N)


def _split_in_proj(w):
    o = 0
    parts = []
    for n in (A_Q_RANK, A_KV_RANK, A_ROPE, 3 * B_HEADS * B_DK + 2 * B_HEADS * B_DV,
              C_HEADS * C_DH, C_KV_HEADS * C_DH, C_KV_HEADS * C_DH, 3 * w.shape[0]):
        parts.append(w[:, o:o + n])
        o += n
    cq, ckv, kr, wb, wcq, wck, wcv, wg = parts
    half = A_ROPE // 2
    wa = jnp.concatenate([cq, ckv, _pad_cols(kr[:, :half], LANES), _pad_cols(kr[:, half:], LANES)], axis=1)
    wc = jnp.concatenate([_group_cols(wcq, C_HEADS, LANES), _group_cols(wck, C_KV_HEADS, LANES),
                          _group_cols(wcv, C_KV_HEADS, LANES)], axis=1)
    return [t.astype(BF16) for t in (wa, wb, wc, wg)]


def _mla_weights(wuq, wukv):
    half = A_ROPE // 2
    wq = wuq.reshape(A_Q_RANK, A_HEADS, A_NOPE + A_ROPE)
    wqn = _group_cols(wq[:, :, :A_NOPE].reshape(A_Q_RANK, -1), A_HEADS, LANES)
    wqr = jnp.concatenate([wq[:, :, A_NOPE:A_NOPE + half].reshape(A_Q_RANK, -1),
                           wq[:, :, A_NOPE + half:].reshape(A_Q_RANK, -1)], axis=1)
    wkv = wukv.reshape(A_KV_RANK, A_HEADS, A_NOPE + A_V)
    wkn = _group_cols(wkv[:, :, :A_NOPE].reshape(A_KV_RANK, -1), A_HEADS, LANES)
    wv = _group_cols(wkv[:, :, A_NOPE:].reshape(A_KV_RANK, -1), A_HEADS, LANES)
    return [t.astype(BF16) for t in (wqn, wqr, wkn, wv)]


def _mla_placement():
    half = A_ROPE // 2
    src = jnp.arange(LANES)[:, None]
    dst = jnp.arange(A_HEADS * LANES)[None, :]
    head, lane = dst // LANES, dst % LANES
    mats = []
    for per_head, base in ((True, A_NOPE), (True, A_NOPE + half), (False, A_NOPE), (False, A_NOPE + half)):
        j = lane - base
        want = head * half + j if per_head else j
        mats.append(((j >= 0) & (j < half) & (src == want)).astype(BF16))
    ones_row = (lane == A_V).astype(F32)
    return jnp.stack(mats), ones_row


def kernel(x, mem, w_in, g_mix, a_gq, a_gkv, a_wuq, a_wukv, b_lb, b_gout, c_sink, rel_bias,
           w_br_a, w_br_b, w_br_c, w_out, g_x, g_mem, x_wq, x_wkv, x_wo, g_ffn,
           f_w1, f_w3, f_w2, g_final):
    Bsz, S, D = x.shape
    depth = w_in.shape[0]
    T = Bsz * S
    M = mem.shape[1]
    tm = ROW_TILE
    half = A_ROPE // 2

    inv = ROPE_THETA ** (-jnp.arange(half, dtype=F32) / half)
    ang = jnp.arange(S, dtype=jnp.int32).astype(F32)[:, None] * inv[None, :]
    cos = jnp.tile(jnp.cos(ang), (1, LANES // half))
    sin = jnp.tile(jnp.sin(ang), (1, LANES // half))
    place, ones_row = _mla_placement()

    sm = jax.nn.softmax(b_lb.astype(F32), axis=1)
    lower = jnp.cumsum(sm, axis=1) - sm[:, :1]

    span = 3 * C_BLOCK
    rel = jnp.arange(span)[None, :] - C_BLOCK - jnp.arange(C_BLOCK)[:, None]
    G = C_HEADS // C_KV_HEADS
    onehot = (_t5_bucket(rel)[None] == jnp.arange(REL_BUCKETS)[:, None, None]).astype(F32)
    bias = jnp.einsum('nh,nqs->hqs', rel_bias.astype(F32), onehot,
                      precision=lax.Precision.HIGHEST).reshape(C_KV_HEADS, G, C_BLOCK, span)
    col = jnp.arange(span)[None, :]
    band = jnp.abs(rel) <= C_WINDOW
    win_mask = jnp.stack([band, band & (col >= C_BLOCK), band & (col < 2 * C_BLOCK)]).astype(F32)
    win_mask = jnp.tile(win_mask, (1, G, 1))

    xt = x.reshape(T, D)
    mem2 = mem.reshape(Bsz * M, D)
    for l in range(depth):
        za, zb, zc, zg = _rms_proj(xt, g_mix[l], _split_in_proj(w_in[l]), (F32, F32, BF16, F32), IN_PROJ_TILE)

        qa, ka, va = _mla_prep(za, a_gq[l], a_gkv[l], *_mla_weights(a_wuq[l], a_wukv[l]),
                               place, ones_row, cos, sin, S, tm)
        shp = (Bsz, S, A_HEADS * LANES)
        ya = _mla_attn(qa.reshape(shp), ka.reshape(shp), va.reshape(shp),
                       MLA_TQ, MLA_SUB, MLA_STEPS).reshape(T, -1)

        of = _hgrn_direction(zb, lower[0, l], None, None, S, reverse=False)
        yb = _hgrn_direction(zb, lower[1, l], of, b_gout[l], S, reverse=True)

        yc = _window_gqa(zc, bias, win_mask, c_sink[l].astype(F32), S)

        xt = _merge(xt, ya, yb, yc, zg,
                    _group_rows(w_br_a[l], A_HEADS, LANES).astype(BF16), w_br_b[l].astype(BF16),
                    w_br_c[l].astype(BF16), w_out[l].astype(BF16), tm)

        (kvm,) = _rms_proj(mem2, g_mem[l], [x_wkv[l].astype(BF16)], (BF16,), Bsz * M)
        xt = _cross(xt, g_x[l], kvm.reshape(Bsz, M, -1), x_wq[l].astype(BF16), x_wo[l].astype(BF16), S, tm)

        xt = _swiglu(xt, g_ffn[l], f_w1[l].astype(BF16), f_w3[l].astype(BF16), f_w2[l].astype(BF16),
                     g_final, l == depth - 1, tm)
    return xt.reshape(Bsz, S, D)
```
